```python
import jax, jax.numpy as jnp
from jax import lax
import numpy as np

D_MODEL = 2048
BATCH = 2
SEQ = 4096
DEPTH = 1
DEC_BATCH = 32
DEC_SEQ = 64
PAST_LEN = 1024

CHUNK = 64
HEAD_DIM = 64
N_HEADS = D_MODEL // HEAD_DIM
D_RWKV = N_HEADS * HEAD_DIM
D_CONV = D_MODEL
CONV_W = 3
LORA_W = 96
LORA_A = 96
LORA_G = 256
D_FF = 5632
RMS_EPS = 1e-6
GN_EPS = 64e-5
N_A = 3 * D_RWKV + LORA_W + LORA_A + LORA_G
N_IN = N_A + 3 * D_CONV + 2 * D_MODEL

kernel_name = 'rwkv7_shortconv_gated_hybrid_stream'


def rms_norm(x, g):
    xf = x.astype(jnp.float32)
    y = xf * lax.rsqrt(jnp.mean(xf * xf, axis=-1, keepdims=True) + RMS_EPS)
    return (y * g.astype(jnp.float32)).astype(x.dtype)


def swiglu(x, w_gate, w_up, w_down):
    return (jax.nn.silu(x @ w_gate) * (x @ w_up)) @ w_down


def wkv7_scan(r, w, k, v, kk, kka, s0):
    def step(s, xs):
        r_t, w_t, k_t, v_t, kk_t, kka_t = xs
        sa = jnp.einsum('bhvk,bhk->bhv', s, kk_t)
        s = s * w_t[:, :, None, :] - sa[..., None] * kka_t[:, :, None, :] + v_t[..., None] * k_t[:, :, None, :]
        return s, jnp.einsum('bhvk,bhk->bhv', s, r_t)
    xs = tuple(jnp.moveaxis(z, 1, 0) for z in (r, w, k, v, kk, kka))
    s, ys = lax.scan(step, s0, xs)
    return jnp.moveaxis(ys, 0, 1), s


def hybrid_layer(x, shift_prev, conv_prev, wkv_prev,
                 ffn1_norm, ffn1_gate, ffn1_up, ffn1_down,
                 mix_norm, w_in, mu_shift, w0, w_lora_up, a0, a_lora_up, g_lora_up,
                 k_k, k_a, r_k, ln_x_g, ln_x_b, conv_w, w_out,
                 ffn2_norm, ffn2_gate, ffn2_up, ffn2_down):
    b, t, _ = x.shape
    f32 = jnp.float32
    x = x + 0.5 * swiglu(rms_norm(x, ffn1_norm), ffn1_gate, ffn1_up, ffn1_down)

    h = rms_norm(x, mix_norm)
    p = h @ w_in
    pa = p[..., :N_A]
    pc = p[..., N_A:N_A + 3 * D_CONV]
    pg = p[..., N_A + 3 * D_CONV:]
    pa_first = (shift_prev @ w_in[:, :N_A])[:, None, :]
    pa_prev = jnp.concatenate([pa_first, pa[:, :-1]], axis=1)
    pa = pa + mu_shift * (pa_prev - pa)
    r, k, v, wl, al, gl = jnp.split(pa, [D_RWKV, 2 * D_RWKV, 3 * D_RWKV,
                                         3 * D_RWKV + LORA_W, 3 * D_RWKV + LORA_W + LORA_A], axis=-1)

    heads = lambda z: z.astype(f32).reshape(b, t, N_HEADS, HEAD_DIM)
    w_log = -jax.nn.softplus(-(w0.astype(f32) + jnp.tanh(wl.astype(f32)) @ w_lora_up.astype(f32))) - 0.5
    decay = jnp.exp(-jnp.exp(w_log))
    a = jax.nn.sigmoid(a0.astype(f32) + al.astype(f32) @ a_lora_up.astype(f32))
    g = jax.nn.sigmoid(gl.astype(f32)) @ g_lora_up.astype(f32)
    kk = heads(k * k_k)
    kk = kk / jnp.maximum(jnp.sqrt(jnp.sum(kk * kk, axis=-1, keepdims=True)), 1e-12)
    k_mod = k.astype(f32) * (1.0 + (a - 1.0) * k_a.astype(f32))
    rh, kh, vh, ah = heads(r), heads(k_mod), heads(v), heads(a)
    o, wkv_new = wkv7_scan(rh, heads(decay), kh, vh, kk, kk * ah, wkv_prev.astype(f32))
    mu = jnp.mean(o, axis=-1, keepdims=True)
    var = jnp.mean(jnp.square(o - mu), axis=-1, keepdims=True)
    o = ((o - mu) * lax.rsqrt(var + GN_EPS)).reshape(b, t, D_RWKV) * ln_x_g.astype(f32) + ln_x_b.astype(f32)
    bonus = (jnp.sum(rh * kh * r_k.astype(f32), axis=-1, keepdims=True) * vh).reshape(b, t, D_RWKV)
    y_a = ((o + bonus) * g).astype(x.dtype)

    gate_b, gate_c, xc = jnp.split(pc, [D_CONV, 2 * D_CONV], axis=-1)
    u = gate_c * xc
    full = jnp.concatenate([conv_prev.astype(u.dtype), u], axis=1)
    conv = conv_w[0] * full[:, 0:t]
    for j in range(1, CONV_W):
        conv = conv + conv_w[j] * full[:, j:j + t]
    y_b = gate_b * conv
    conv_new = full[:, t:]

    ga, gb = jnp.split(pg, [D_MODEL], axis=-1)
    merged = jax.nn.sigmoid(ga) * y_a + jax.nn.sigmoid(gb) * y_b
    x = x + merged @ w_out

    x = x + 0.5 * swiglu(rms_norm(x, ffn2_norm), ffn2_gate, ffn2_up, ffn2_down)
    return x, h[:, -1], conv_new, wkv_new.astype(wkv_prev.dtype)


def setup_inputs(seed: int = 0) -> dict:
    key = jax.random.key(seed)
    ks = jax.random.split(key, 32)
    f32 = jnp.float32
    L = DEPTH
    nrm = lambda k, shape, scale: jax.random.normal(k, shape, f32) * scale
    return {
        'x_prompt': nrm(ks[0], (BATCH, SEQ, D_MODEL), 1.0),
        'x_sample': nrm(ks[1], (DEC_BATCH, DEC_SEQ, D_MODEL), 1.0),
        'state_shift': nrm(ks[2], (L, DEC_BATCH, D_MODEL), 1.0),
        'state_conv': nrm(ks[3], (L, DEC_BATCH, CONV_W - 1, D_CONV), 1.0),
        'state_wkv': nrm(ks[4], (L, DEC_BATCH, N_HEADS, HEAD_DIM, HEAD_DIM), 0.5),
        'ffn1_norm': 1.0 + nrm(ks[5], (L, D_MODEL), 0.01),
        'ffn1_gate': nrm(ks[6], (L, D_MODEL, D_FF), D_MODEL ** -0.5),
        'ffn1_up': nrm(ks[7], (L, D_MODEL, D_FF), D_MODEL ** -0.5),
        'ffn1_down': nrm(ks[8], (L, D_FF, D_MODEL), D_FF ** -0.5),
        'mix_norm': 1.0 + nrm(ks[9], (L, D_MODEL), 0.01),
        'w_in': nrm(ks[10], (L, D_MODEL, N_IN), D_MODEL ** -0.5),
        'mu_shift': jax.random.uniform(ks[11], (L, N_A), f32),
        'w0': jax.random.uniform(ks[12], (L, D_RWKV), f32, -5.0, 0.0),
        'w_lora_up': nrm(ks[13], (L, LORA_W, D_RWKV), LORA_W ** -0.5),
        'a0': nrm(ks[14], (L, D_RWKV), 0.1),
        'a_lora_up': nrm(ks[15], (L, LORA_A, D_RWKV), LORA_A ** -0.5),
        'g_lora_up': nrm(ks[16], (L, LORA_G, D_RWKV), LORA_G ** -0.5),
        'k_k': 0.85 + nrm(ks[17], (L, D_RWKV), 0.05),
        'k_a': 1.0 + nrm(ks[18], (L, D_RWKV), 0.05),
        'r_k': nrm(ks[19], (L, N_HEADS, HEAD_DIM), 0.1),
        'ln_x_g': 1.0 + nrm(ks[20], (L, D_RWKV), 0.01),
        'ln_x_b': nrm(ks[21], (L, D_RWKV), 0.01),
        'conv_w': nrm(ks[22], (L, CONV_W, D_CONV), CONV_W ** -0.5),
        'w_out': nrm(ks[23], (L, D_MODEL, D_MODEL), D_MODEL ** -0.5),
        'ffn2_norm': 1.0 + nrm(ks[24], (L, D_MODEL), 0.01),
        'ffn2_gate': nrm(ks[25], (L, D_MODEL, D_FF), D_MODEL ** -0.5),
        'ffn2_up': nrm(ks[26], (L, D_MODEL, D_FF), D_MODEL ** -0.5),
        'ffn2_down': nrm(ks[27], (L, D_FF, D_MODEL), D_FF ** -0.5),
        'final_norm': 1.0 + nrm(ks[28], (D_MODEL,), 0.01),
    }


def reference(x_prompt, x_sample, state_shift, state_conv, state_wkv,
              ffn1_norm, ffn1_gate, ffn1_up, ffn1_down,
              mix_norm, w_in, mu_shift, w0, w_lora_up, a0, a_lora_up, g_lora_up,
              k_k, k_a, r_k, ln_x_g, ln_x_b, conv_w, w_out,
              ffn2_norm, ffn2_gate, ffn2_up, ffn2_down, final_norm):
    hp, hs = x_prompt, x_sample
    sh_p, cv_p, wk_p, sh_s, cv_s, wk_s = [], [], [], [], [], []
    for l in range(DEPTH):
        lw = tuple(w[l] for w in (ffn1_norm, ffn1_gate, ffn1_up, ffn1_down,
                                  mix_norm, w_in, mu_shift, w0, w_lora_up, a0, a_lora_up, g_lora_up,
                                  k_k, k_a, r_k, ln_x_g, ln_x_b, conv_w, w_out,
                                  ffn2_norm, ffn2_gate, ffn2_up, ffn2_down))
        zs = jnp.zeros((BATCH, D_MODEL), x_prompt.dtype)
        zc = jnp.zeros((BATCH, CONV_W - 1, D_CONV), x_prompt.dtype)
        zw = jnp.zeros((BATCH, N_HEADS, HEAD_DIM, HEAD_DIM), state_wkv.dtype)
        hp, s1, c1, w1 = hybrid_layer(hp, zs, zc, zw, *lw)
        hs, s2, c2, w2 = hybrid_layer(hs, state_shift[l], state_conv[l], state_wkv[l], *lw)
        sh_p.append(s1); cv_p.append(c1); wk_p.append(w1)
        sh_s.append(s2); cv_s.append(c2); wk_s.append(w2)
    y_prompt = rms_norm(hp, final_norm)
    y_sample = rms_norm(hs, final_norm)
    return (y_prompt, y_sample,
            jnp.stack(sh_p), jnp.stack(cv_p), jnp.stack(wk_p),
            jnp.stack(sh_s), jnp.stack(cv_s), jnp.stack(wk_s))
```

```python
import functools
import math

import jax
import jax.numpy as jnp
from jax import lax
from jax.experimental import pallas as pl
from jax.experimental.pallas import tpu as pltpu

F32 = jnp.float32
BF16 = jnp.bfloat16

HEAD_DIM = 64
N_HEADS = 32
D_MODEL = N_HEADS * HEAD_DIM
CONV_W = 3
LORA_W = 96
LORA_A = 96
LORA_G = 256
RMS_EPS = 1e-6
GN_EPS = 64e-5
EXP_M05 = math.exp(-0.5)

LANES = 128
SUBLANES = 8
N_SLABS = D_MODEL // LANES
K_PER_SLAB = LANES // N_HEADS
LORA_PAD = 512
N_GROUPS = 8
N_PROJ = N_GROUPS * D_MODEL + LORA_PAD
VMEM_LIMIT = 56 * 1024 * 1024


def _cparams(sem):
    return pltpu.CompilerParams(dimension_semantics=sem, vmem_limit_bytes=VMEM_LIMIT)


def _sigmoid(x):
    return 1.0 / (1.0 + jnp.exp(-x))


def _ffn_body(x_ref, g1_ref, wg_ref, wu_ref, wd_ref, g2_ref, *rest, emit_x):
    if emit_x:
        xo_ref, ho_ref, xn_ref, acc_ref = rest
    else:
        ho_ref, xn_ref, acc_ref = rest
    j = pl.program_id(1)

    @pl.when(j == 0)
    def _():
        x = x_ref[...]
        ms = jnp.mean(x * x, axis=-1, keepdims=True)
        xn_ref[...] = (x * lax.rsqrt(ms + RMS_EPS) * g1_ref[...]).astype(BF16)
        acc_ref[...] = jnp.zeros_like(acc_ref)

    xn = xn_ref[...]
    gate = jnp.dot(xn, wg_ref[...], preferred_element_type=F32)
    up = jnp.dot(xn, wu_ref[...], preferred_element_type=F32)
    act = (gate * _sigmoid(gate) * up).astype(BF16)
    acc_ref[...] += jnp.dot(act, wd_ref[...], preferred_element_type=F32)

    @pl.when(j == pl.num_programs(1) - 1)
    def _():
        x1 = x_ref[...] + 0.5 * acc_ref[...]
        if emit_x:
            xo_ref[...] = x1
        ms = jnp.mean(x1 * x1, axis=-1, keepdims=True)
        ho_ref[...] = x1 * lax.rsqrt(ms + RMS_EPS) * g2_ref[...]


def _ffn(x, g1, wg, wu, wd, g2, *, emit_x, tm=512, tf=512):
    m, d = x.shape
    f = wg.shape[1]
    grid = (m // tm, f // tf)
    row = pl.BlockSpec((tm, d), lambda i, j: (i, 0))
    vec = pl.BlockSpec((1, d), lambda i, j: (0, 0))
    out_shape = [jax.ShapeDtypeStruct((m, d), F32)] * (2 if emit_x else 1)
    return pl.pallas_call(
        functools.partial(_ffn_body, emit_x=emit_x),
        grid=grid,
        in_specs=[row, vec,
                  pl.BlockSpec((d, tf), lambda i, j: (0, j)),
                  pl.BlockSpec((d, tf), lambda i, j: (0, j)),
                  pl.BlockSpec((tf, d), lambda i, j: (j, 0)),
                  vec],
        out_specs=[row] * (2 if emit_x else 1),
        out_shape=out_shape,
        scratch_shapes=[pltpu.VMEM((tm, d), BF16), pltpu.VMEM((tm, d), F32)],
        compiler_params=_cparams(("parallel", "arbitrary")),
        name="ffn",
    )(x, g1, wg, wu, wd, g2)


def _proj_body(h_ref, w_ref, o_ref, hb_ref):
    @pl.when(pl.program_id(1) == 0)
    def _():
        hb_ref[...] = h_ref[...].astype(BF16)

    o_ref[...] = jnp.dot(hb_ref[...], w_ref[...], preferred_element_type=F32)


def _proj(h, w, *, tm, tn=512):
    m, d = h.shape
    n = w.shape[1]
    return pl.pallas_call(
        _proj_body,
        grid=(m // tm, n // tn),
        in_specs=[pl.BlockSpec((tm, d), lambda i, j: (i, 0)),
                  pl.BlockSpec((d, tn), lambda i, j: (0, j))],
        out_specs=pl.BlockSpec((tm, tn), lambda i, j: (i, j)),
        out_shape=jax.ShapeDtypeStruct((m, n), F32),
        scratch_shapes=[pltpu.VMEM((tm, d), BF16)],
        compiler_params=_cparams(("parallel", "arbitrary")),
        name="proj",
    )(h, w)


def _slab(x, q):
    return x[:, q * LANES:(q + 1) * LANES]


def _head_sum(x):
    acc = _slab(x, 0)
    for q in range(1, N_SLABS):
        acc = acc + _slab(x, q)
    acc = acc + pltpu.roll(acc, 2 * N_HEADS, axis=1)
    return acc + pltpu.roll(acc, N_HEADS, axis=1)


def _tile_slabs(x):
    return jnp.concatenate([x] * N_SLABS, axis=1)


def _prep_body(pr_ref, pk_ref, pv_ref, pl_ref, br_ref, bk_ref, bv_ref, bl_ref,
               mur_ref, muk_ref, muv_ref, mul_ref, w0_ref, a0_ref, kkp_ref, kap_ref, rkp_ref,
               wup_ref, aup_ref, gup_ref,
               r_o, w_o, k_o, v_o, kk_o, kka_o, g_o, bc_o, *, seg):
    tm = pr_ref.shape[0]
    nseg = tm // seg
    row = lax.broadcasted_iota(jnp.int32, (tm, 1), 0)

    def shifted(p_ref, b_ref, mu_ref):
        p = p_ref[...]
        prev = pltpu.roll(p, 1, axis=0)
        for s in range(nseg):
            prev = jnp.where(row == s * seg, b_ref[s:s + 1, :], prev)
        return p + mu_ref[...] * (prev - p)

    r = shifted(pr_ref, br_ref, mur_ref)
    k = shifted(pk_ref, bk_ref, muk_ref)
    v = shifted(pv_ref, bv_ref, muv_ref)
    lo = shifted(pl_ref, bl_ref, mul_ref)

    wl = jnp.tanh(lo[:, 0:LANES]).astype(BF16)
    al = lo[:, LANES:2 * LANES].astype(BF16)
    gl = _sigmoid(lo[:, 2 * LANES:LORA_PAD]).astype(BF16)
    z = w0_ref[...] + jnp.dot(wl, wup_ref[...], preferred_element_type=F32)
    decay = jnp.exp(-EXP_M05 * _sigmoid(z))
    a = _sigmoid(a0_ref[...] + jnp.dot(al, aup_ref[...], preferred_element_type=F32))
    g = jnp.dot(gl, gup_ref[...], preferred_element_type=F32)

    kk = k * kkp_ref[...]
    norm = jnp.maximum(jnp.sqrt(_head_sum(kk * kk)), 1e-12)
    kk = kk * _tile_slabs(1.0 / norm)
    k_mod = k * (1.0 + (a - 1.0) * kap_ref[...])

    r_o[...] = r
    w_o[...] = decay
    k_o[...] = k_mod
    v_o[...] = v
    kk_o[...] = kk
    kka_o[...] = kk * a
    g_o[...] = g
    bc_o[...] = _head_sum(r * k_mod * rkp_ref[...])


def _prep(p, bnd, vecs, ups, *, tm, seg):
    m = p.shape[0]
    d = D_MODEL
    lora_blk = N_GROUPS * d // LORA_PAD
    grp = lambda g: pl.BlockSpec((tm, d), lambda i, g=g: (i, g))
    bgrp = lambda g: pl.BlockSpec((SUBLANES, d), lambda i, g=g: (i, g))
    vec = pl.BlockSpec((1, d), lambda i: (0, 0))
    row_out = pl.BlockSpec((tm, d), lambda i: (i, 0))
    mur, muk, muv, mul, w0, a0, kkp, kap, rkp = vecs
    wup, aup, gup = ups
    return pl.pallas_call(
        functools.partial(_prep_body, seg=seg),
        grid=(m // tm,),
        in_specs=[grp(0), grp(1), grp(2),
                  pl.BlockSpec((tm, LORA_PAD), lambda i: (i, lora_blk)),
                  bgrp(0), bgrp(1), bgrp(2),
                  pl.BlockSpec((SUBLANES, LORA_PAD), lambda i: (i, 3 * d // LORA_PAD)),
                  vec, vec, vec, pl.BlockSpec((1, LORA_PAD), lambda i: (0, 0)),
                  vec, vec, vec, vec, vec,
                  pl.BlockSpec((LANES, d), lambda i: (0, 0)),
                  pl.BlockSpec((LANES, d), lambda i: (0, 0)),
                  pl.BlockSpec((LORA_G, d), lambda i: (0, 0))],
        out_specs=[row_out] * 7 + [pl.BlockSpec((tm, LANES), lambda i: (i, 0))],
        out_shape=[jax.ShapeDtypeStruct((m, d), F32)] * 7 + [jax.ShapeDtypeStruct((m, LANES), F32)],
        compiler_params=_cparams(("parallel",)),
        name="prep",
    )(p, p, p, p, bnd, bnd, bnd, bnd, mur, muk, muv, mul, w0, a0, kkp, kap, rkp, wup, aup, gup)


N_VT = HEAD_DIM // SUBLANES
N_KQ = N_SLABS
PAIR = 2


def _group_sum(x):
    x = x + pltpu.roll(x, 2 * N_HEADS, axis=1)
    return x + pltpu.roll(x, N_HEADS, axis=1)


def _scan_body(r_ref, w_ref, k_ref, v_ref, kk_ref, kka_ref, s0_ref, y_ref, s_ref, sa_ref, *, tt):
    @pl.when(pl.program_id(1) == 0)
    def _():
        s_ref[...] = s0_ref[...]

    sub = lax.broadcasted_iota(jnp.int32, (SUBLANES, LANES), 0)
    grp = lax.broadcasted_iota(jnp.int32, (SUBLANES, LANES), 1) // N_HEADS
    rot = (grp - sub % K_PER_SLAB) % K_PER_SLAB
    upper = sub >= K_PER_SLAB
    diag = grp == sub % K_PER_SLAB

    def tile8(ref, b, t8):
        return ref.at[b, pl.ds(pl.multiple_of(t8 * SUBLANES, SUBLANES), SUBLANES), :]

    def row(view, s, q):
        return jnp.broadcast_to(view[pl.ds(s, 1), pl.ds(q * LANES, LANES)], (SUBLANES, LANES))

    for b in range(PAIR):
        kk0 = tile8(kk_ref, b, 0)
        for j in range(N_VT):
            acc = s_ref[b, j, 0] * row(kk0, 0, 0)
            for kq in range(1, N_KQ):
                acc = acc + s_ref[b, j, kq] * row(kk0, 0, kq)
            sa_ref[b, j] = _group_sum(acc)

    def step(b, views, s):
        r_v, w_v, k_v, v_v, kk_v, kka_v, kkn_v, y_v = views
        vcols = []
        for j in range(N_VT):
            base = jnp.where(upper, row(v_v, s, 2 * j + 1), row(v_v, s, 2 * j))
            col = base
            for d in range(1, K_PER_SLAB):
                col = jnp.where(rot == d, pltpu.roll(base, d * N_HEADS, axis=1), col)
            vcols.append(col)
        sas = [sa_ref[b, j] for j in range(N_VT)]
        acc_y = [None] * N_VT
        acc_s = [None] * N_VT
        for kq in range(N_KQ):
            w_row = row(w_v, s, kq)
            kka_row = row(kka_v, s, kq)
            k_row = row(k_v, s, kq)
            r_row = row(r_v, s, kq)
            kkn_row = row(kk_v, s + 1, kq) if s + 1 < SUBLANES else row(kkn_v, 0, kq)
            for j in range(N_VT):
                s_new = s_ref[b, j, kq] * w_row - sas[j] * kka_row + vcols[j] * k_row
                s_ref[b, j, kq] = s_new
                ty = s_new * r_row
                ts = s_new * kkn_row
                acc_y[j] = ty if kq == 0 else acc_y[j] + ty
                acc_s[j] = ts if kq == 0 else acc_s[j] + ts
        for j in range(N_VT):
            sa_ref[b, j] = _group_sum(acc_s[j])
            yv = jnp.where(diag, _group_sum(acc_y[j]), 0.0)
            yv = yv + pltpu.roll(yv, 1, axis=0)
            yv = yv + pltpu.roll(yv, 2, axis=0)
            y_v[pl.ds(s, 1), pl.ds(2 * j * LANES, LANES)] = yv[K_PER_SLAB - 1:K_PER_SLAB, :]
            y_v[pl.ds(s, 1), pl.ds((2 * j + 1) * LANES, LANES)] = yv[SUBLANES - 1:SUBLANES, :]

    n8 = tt // SUBLANES

    def steps8(i, carry):
        t8 = i // PAIR
        b = i % PAIR
        t8n = jnp.minimum(t8 + 1, n8 - 1)
        views = tuple(tile8(ref, b, t8) for ref in (r_ref, w_ref, k_ref, v_ref, kk_ref, kka_ref))
        views += (tile8(kk_ref, b, t8n), tile8(y_ref, b, t8))
        for s in range(SUBLANES):
            step(b, views, s)
        return carry

    lax.fori_loop(0, n8 * PAIR, steps8, 0)


def _scan(r, w, k, v, kk, kka, s0, *, tt):
    bsz, t_len, d = r.shape
    seq = pl.BlockSpec((PAIR, tt, d), lambda i, j: (i, j, 0))
    st = pl.BlockSpec((PAIR, N_VT, N_KQ, SUBLANES, LANES), lambda i, j: (i, 0, 0, 0, 0))
    return pl.pallas_call(
        functools.partial(_scan_body, tt=tt),
        grid=(bsz // PAIR, t_len // tt),
        in_specs=[seq] * 6 + [st],
        out_specs=[seq, st],
        out_shape=[jax.ShapeDtypeStruct((bsz, t_len, d), F32),
                   jax.ShapeDtypeStruct(s0.shape, F32)],
        scratch_shapes=[pltpu.VMEM((PAIR, N_VT, SUBLANES, LANES), F32)],
        compiler_params=_cparams(("parallel", "arbitrary")),
        name="wkv_scan",
    )(r, w, k, v, kk, kka, s0)


def _state_to_tiles(s):
    b = s.shape[0]
    s = s.reshape(b, N_HEADS, N_VT, SUBLANES, N_KQ, K_PER_SLAB)
    return s.transpose(0, 2, 4, 3, 5, 1).reshape(b, N_VT, N_KQ, SUBLANES, LANES)


def _tiles_to_state(s):
    b = s.shape[0]
    s = s.reshape(b, N_VT, N_KQ, SUBLANES, K_PER_SLAB, N_HEADS)
    return s.transpose(0, 5, 1, 3, 2, 4).reshape(b, N_HEADS, HEAD_DIM, HEAD_DIM)


def _post_body(y_ref, v_ref, bc_ref, g_ref, gtb_ref, gtc_ref, xc_ref, ga_ref, gb_ref, x1_ref,
               bgc_ref, bxc_ref, lng_ref, lnb_ref, cw_ref, wout_ref, x2_o, ut_o, *, seg):
    tm = y_ref.shape[0]
    nseg = tm // seg
    row = lax.broadcasted_iota(jnp.int32, (tm, 1), 0)
    inv_n = 1.0 / HEAD_DIM

    o = y_ref[...]
    mu = _head_sum(o) * inv_n
    c = o - _tile_slabs(mu)
    var = _head_sum(c * c) * inv_n
    o = c * _tile_slabs(lax.rsqrt(var + GN_EPS)) * lng_ref[...] + lnb_ref[...]
    bonus = _tile_slabs(bc_ref[...]) * v_ref[...]
    y_a = (o + bonus) * g_ref[...]

    u = gtc_ref[...] * xc_ref[...]
    ub = bgc_ref[...] * bxc_ref[...]
    u1 = pltpu.roll(u, 1, axis=0)
    u2 = pltpu.roll(u, 2, axis=0)
    for s in range(nseg):
        u1 = jnp.where(row == s * seg, ub[2 * s + 1:2 * s + 2, :], u1)
        u2 = jnp.where(row == s * seg, ub[2 * s:2 * s + 1, :], u2)
        u2 = jnp.where(row == s * seg + 1, ub[2 * s + 1:2 * s + 2, :], u2)
    conv = cw_ref[0:1, :] * u2 + cw_ref[1:2, :] * u1 + cw_ref[2:3, :] * u
    y_b = gtb_ref[...] * conv

    merged = _sigmoid(ga_ref[...]) * y_a + _sigmoid(gb_ref[...]) * y_b
    x2_o[...] = x1_ref[...] + jnp.dot(merged.astype(BF16), wout_ref[...], preferred_element_type=F32)

    ut_o[...] = jnp.zeros_like(ut_o)
    for s in range(nseg):
        ut_o[2 * s:2 * s + 2, :] = u[(s + 1) * seg - 2:(s + 1) * seg, :]


def _post(y, v, bc, g, p, x1, bgc, bxc, lng, lnb, cw, wout, *, tm, seg):
    m, d = y.shape
    rowb = pl.BlockSpec((tm, d), lambda i: (i, 0))
    grp = lambda gi: pl.BlockSpec((tm, d), lambda i, gi=gi: (i, gi))
    vec = pl.BlockSpec((1, d), lambda i: (0, 0))
    bnd = pl.BlockSpec((SUBLANES, d), lambda i: (i, 0))
    return pl.pallas_call(
        functools.partial(_post_body, seg=seg),
        grid=(m // tm,),
        in_specs=[rowb, rowb, pl.BlockSpec((tm, LANES), lambda i: (i, 0)), rowb,
                  grp(3), grp(4), grp(5), grp(6), grp(7), rowb,
                  bnd, bnd, vec, vec,
                  pl.BlockSpec((CONV_W, d), lambda i: (0, 0)),
                  pl.BlockSpec((d, d), lambda i: (0, 0), pipeline_mode=pl.Buffered(1))],
        out_specs=[rowb, bnd],
        out_shape=[jax.ShapeDtypeStruct((m, d), F32),
                   jax.ShapeDtypeStruct((m // tm * SUBLANES, d), F32)],
        compiler_params=_cparams(("parallel",)),
        name="post",
    )(y, v, bc, g, p, p, p, p, p, x1, bgc, bxc, lng, lnb, cw, wout)


def _hm(x):
    lead = x.shape[:-1]
    return x.reshape(*lead, N_HEADS, HEAD_DIM).swapaxes(-1, -2).reshape(*lead, D_MODEL)


def _hm_inv(x):
    lead = x.shape[:-1]
    return x.reshape(*lead, HEAD_DIM, N_HEADS).swapaxes(-1, -2).reshape(*lead, D_MODEL)


def _pad_lora(x):
    z = jnp.zeros(x.shape[:-1] + (LANES - LORA_W,), x.dtype)
    return jnp.concatenate([x[..., :LORA_W], z, x[..., LORA_W:LORA_W + LORA_A], z,
                            x[..., LORA_W + LORA_A:]], axis=-1)


def _pad_rows(x, rows):
    return jnp.concatenate([x, jnp.zeros((rows - x.shape[0],) + x.shape[1:], x.dtype)], axis=0)


def _tile_bounds(rows, n_tiles, n_used):
    w = rows.shape[-1]
    rows = rows.reshape(n_tiles, n_used, w)
    pad = jnp.zeros((n_tiles, SUBLANES - n_used, w), rows.dtype)
    return jnp.concatenate([rows, pad], axis=1).reshape(n_tiles * SUBLANES, w)


def _layer(x, shift_prev, conv_prev, wkv_prev, wts, *, tm_prep, tm_post, seg_prep, seg_post, tt):
    bsz, t_len, d = x.shape
    m = bsz * t_len
    xf = x.reshape(m, d)

    x1, h = _ffn(xf, wts["ffn1_norm"], wts["ffn1_gate"], wts["ffn1_up"], wts["ffn1_down"],
                 wts["mix_norm"], emit_x=True)
    p = _proj(h, wts["w_all"], tm=512)

    n_rkv = 3 * d
    if shift_prev is None:
        n_tiles = m // tm_prep
        starts = jnp.arange(n_tiles) * tm_prep
        prev = jnp.take(p, jnp.maximum(starts - 1, 0), axis=0)
        prev = jnp.where((starts % t_len == 0)[:, None], 0.0, prev)
        n_used = 1
    else:
        rows = shift_prev.shape[0]
        prev = _proj(_pad_rows(shift_prev, 32), wts["w_all"], tm=32)[:rows]
        n_tiles = m // tm_prep
        n_used = tm_prep // seg_prep
    bnd = jnp.concatenate([prev[:, :n_rkv], prev[:, N_GROUPS * d:]], axis=1)
    bnd = _tile_bounds(bnd, n_tiles, n_used)

    r, w, k, v, kk, kka, g, bc = _prep(p, bnd, wts["prep_vecs"], wts["prep_ups"],
                                       tm=tm_prep, seg=seg_prep)

    s0 = _state_to_tiles(wkv_prev)
    sh = (bsz, t_len, d)
    y, s_new = _scan(r.reshape(sh), w.reshape(sh), k.reshape(sh), v.reshape(sh),
                     kk.reshape(sh), kka.reshape(sh), s0, tt=tt)
    wkv_new = _tiles_to_state(s_new)

    n_tiles = m // tm_post
    if conv_prev is None:
        starts = jnp.arange(n_tiles) * tm_post
        idx = jnp.maximum(starts[:, None] + jnp.array([-2, -1])[None, :], 0).reshape(-1)
        rows = jnp.take(p, idx, axis=0)
        live = jnp.repeat(starts % t_len != 0, 2)[:, None]
        bgc = jnp.where(live, rows[:, 4 * d:5 * d], 0.0)
        bxc = rows[:, 5 * d:6 * d]
        n_used = 2
    else:
        bgc = _hm(conv_prev).reshape(bsz * (CONV_W - 1), d)
        bxc = jnp.ones_like(bgc)
        n_used = 2 * (tm_post // seg_post)
    bgc = _tile_bounds(bgc, n_tiles, n_used)
    bxc = _tile_bounds(bxc, n_tiles, n_used)

    x2, utail = _post(y.reshape(m, d), v, bc, g, p, x1, bgc, bxc,
                      wts["ln_g"], wts["ln_b"], wts["conv_w"], wts["w_out"],
                      tm=tm_post, seg=seg_post)

    (yout,) = _ffn(x2, wts["ffn2_norm"], wts["ffn2_gate"], wts["ffn2_up"], wts["ffn2_down"],
                   wts["final_norm"], emit_x=False)

    shift_new = h.reshape(bsz, t_len, d)[:, -1]
    nseg_post = tm_post // seg_post
    ut = utail.reshape(n_tiles, SUBLANES, d)[:, :2 * nseg_post]
    ut = ut.reshape(n_tiles * nseg_post, 2, d)
    segs_per_seq = t_len // seg_post
    conv_new = _hm_inv(ut[segs_per_seq - 1::segs_per_seq])
    return yout.reshape(bsz, t_len, d), shift_new, conv_new, wkv_new


def kernel(x_prompt, x_sample, state_shift, state_conv, state_wkv, ffn1_norm, ffn1_gate, ffn1_up, ffn1_down, mix_norm, w_in, mu_shift, w0, w_lora_up, a0, a_lora_up, g_lora_up, k_k, k_a, r_k, ln_x_g, ln_x_b, conv_w, w_out, ffn2_norm, ffn2_gate, ffn2_up, ffn2_down, final_norm):
    depth = w_in.shape[0]
    assert depth == 1, "single-layer kernel"
    d = D_MODEL
    n_a = 3 * d + LORA_W + LORA_A + LORA_G
    l = 0

    wi = w_in[l]
    groups = [wi[:, 0:d], wi[:, d:2 * d], wi[:, 2 * d:3 * d]]
    groups += [wi[:, n_a + i * d:n_a + (i + 1) * d] for i in range(5)]
    w_all = jnp.concatenate([_hm(gw) for gw in groups] + [_pad_lora(wi[:, 3 * d:n_a])], axis=1).astype(BF16)

    mu = mu_shift[l]
    row = lambda x: x.reshape(1, -1)
    prep_vecs = (row(_hm(mu[0:d])), row(_hm(mu[d:2 * d])), row(_hm(mu[2 * d:3 * d])),
                 row(_pad_lora(mu[3 * d:])),
                 row(_hm(w0[l])), row(_hm(a0[l])), row(_hm(k_k[l])), row(_hm(k_a[l])),
                 row(_hm(r_k[l].reshape(d))))
    prep_ups = (_pad_rows(_hm(w_lora_up[l]), LANES).astype(BF16),
                _pad_rows(_hm(a_lora_up[l]), LANES).astype(BF16),
                _hm(g_lora_up[l]).astype(BF16))

    wts = dict(
        ffn1_norm=row(ffn1_norm[l]), ffn1_gate=ffn1_gate[l].astype(BF16), ffn1_up=ffn1_up[l].astype(BF16),
        ffn1_down=ffn1_down[l].astype(BF16), mix_norm=row(mix_norm[l]), w_all=w_all,
        prep_vecs=prep_vecs, prep_ups=prep_ups,
        ln_g=row(_hm(ln_x_g[l])), ln_b=row(_hm(ln_x_b[l])), conv_w=_hm(conv_w[l]),
        w_out=_hm(w_out[l].T).T.astype(BF16),
        ffn2_norm=row(ffn2_norm[l]), ffn2_gate=ffn2_gate[l].astype(BF16), ffn2_up=ffn2_up[l].astype(BF16),
        ffn2_down=ffn2_down[l].astype(BF16), final_norm=row(final_norm),
    )

    b_p, t_p, _ = x_prompt.shape
    b_s, t_s, _ = x_sample.shape
    zero_state = jnp.zeros((b_p, N_HEADS, HEAD_DIM, HEAD_DIM), state_wkv.dtype)
    y_p, sh_p, cv_p, wk_p = _layer(x_prompt, None, None, zero_state, wts,
                                   tm_prep=128, tm_post=256, seg_prep=128, seg_post=256, tt=64)
    y_s, sh_s, cv_s, wk_s = _layer(x_sample, state_shift[l], state_conv[l], state_wkv[l], wts,
                                   tm_prep=128, tm_post=256, seg_prep=t_s, seg_post=t_s, tt=t_s)
    return (y_p, y_s, sh_p[None], cv_p[None], wk_p[None], sh_s[None], cv_s[None], wk_s[None])
```

```python
import functools
import math

import jax
import jax.numpy as jnp
from jax import lax
from jax.experimental import pallas as pl
from jax.experimental.pallas import tpu as pltpu

F32 = jnp.float32
BF16 = jnp.bfloat16

HEAD_DIM = 64
N_HEADS = 32
D_MODEL = N_HEADS * HEAD_DIM
CONV_W = 3
LORA_W = 96
LORA_A = 96
LORA_G = 256
RMS_EPS = 1e-6
GN_EPS = 64e-5
EXP_M05 = math.exp(-0.5)

LANES = 128
SUBLANES = 8
N_SLABS = D_MODEL // LANES
K_PER_SLAB = LANES // N_HEADS
LORA_PAD = 512
N_GROUPS = 8
N_PROJ = N_GROUPS * D_MODEL + LORA_PAD
VMEM_LIMIT = 56 * 1024 * 1024


def _cparams(sem):
    return pltpu.CompilerParams(dimension_semantics=sem, vmem_limit_bytes=VMEM_LIMIT)


def _sigmoid(x):
    return 1.0 / (1.0 + jnp.exp(-x))


def _ffn_body(x_ref, g1_ref, wg_ref, wu_ref, wd_ref, g2_ref, *rest, emit_x):
    if emit_x:
        xo_ref, ho_ref, xn_ref, acc_ref = rest
    else:
        ho_ref, xn_ref, acc_ref = rest
    j = pl.program_id(1)

    @pl.when(j == 0)
    def _():
        x = x_ref[...]
        ms = jnp.mean(x * x, axis=-1, keepdims=True)
        xn_ref[...] = (x * lax.rsqrt(ms + RMS_EPS) * g1_ref[...]).astype(BF16)
        acc_ref[...] = jnp.zeros_like(acc_ref)

    xn = xn_ref[...]
    gate = jnp.dot(xn, wg_ref[...], preferred_element_type=F32)
    up = jnp.dot(xn, wu_ref[...], preferred_element_type=F32)
    act = (gate * _sigmoid(gate) * up).astype(BF16)
    acc_ref[...] += jnp.dot(act, wd_ref[...], preferred_element_type=F32)

    @pl.when(j == pl.num_programs(1) - 1)
    def _():
        x1 = x_ref[...] + 0.5 * acc_ref[...]
        if emit_x:
            xo_ref[...] = x1
        ms = jnp.mean(x1 * x1, axis=-1, keepdims=True)
        ho_ref[...] = x1 * lax.rsqrt(ms + RMS_EPS) * g2_ref[...]


def _ffn(x, g1, wg, wu, wd, g2, *, emit_x, tm=512, tf=512):
    m, d = x.shape
    f = wg.shape[1]
    grid = (m // tm, f // tf)
    row = pl.BlockSpec((tm, d), lambda i, j: (i, 0))
    vec = pl.BlockSpec((1, d), lambda i, j: (0, 0))
    out_shape = [jax.ShapeDtypeStruct((m, d), F32)] * (2 if emit_x else 1)
    return pl.pallas_call(
        functools.partial(_ffn_body, emit_x=emit_x),
        grid=grid,
        in_specs=[row, vec,
                  pl.BlockSpec((d, tf), lambda i, j: (0, j)),
                  pl.BlockSpec((d, tf), lambda i, j: (0, j)),
                  pl.BlockSpec((tf, d), lambda i, j: (j, 0)),
                  vec],
        out_specs=[row] * (2 if emit_x else 1),
        out_shape=out_shape,
        scratch_shapes=[pltpu.VMEM((tm, d), BF16), pltpu.VMEM((tm, d), F32)],
        compiler_params=_cparams(("parallel", "arbitrary")),
        name="ffn",
    )(x, g1, wg, wu, wd, g2)


def _proj_body(h_ref, *refs, starts):
    w_refs = refs[:len(starts) - 1]
    o_ref, hb_ref = refs[len(starts) - 1:]
    j = pl.program_id(1)

    @pl.when(j == 0)
    def _():
        hb_ref[...] = h_ref[...].astype(BF16)

    for g, w_ref in enumerate(w_refs):
        @pl.when((j >= starts[g]) & (j < starts[g + 1]))
        def _(w_ref=w_ref):
            o_ref[...] = jnp.dot(hb_ref[...], w_ref[...], preferred_element_type=F32)


def _proj(h, ws, *, tm, tn=256):
    m, d = h.shape
    starts = [0]
    for w in ws:
        starts.append(starts[-1] + w.shape[1] // tn)
    w_specs = [pl.BlockSpec((d, tn), lambda i, j, lo=lo, hi=hi: (0, jnp.clip(j - lo, 0, hi - lo - 1)))
               for lo, hi in zip(starts[:-1], starts[1:])]
    return pl.pallas_call(
        functools.partial(_proj_body, starts=tuple(starts)),
        grid=(m // tm, starts[-1]),
        in_specs=[pl.BlockSpec((tm, d), lambda i, j: (i, 0))] + w_specs,
        out_specs=pl.BlockSpec((tm, tn), lambda i, j: (i, j)),
        out_shape=jax.ShapeDtypeStruct((m, starts[-1] * tn), F32),
        scratch_shapes=[pltpu.VMEM((tm, d), BF16)],
        compiler_params=_cparams(("parallel", "arbitrary")),
        name="proj",
    )(h, *ws)


def _slab(x, q):
    return x[:, q * LANES:(q + 1) * LANES]


def _head_sum(x):
    acc = _slab(x, 0)
    for q in range(1, N_SLABS):
        acc = acc + _slab(x, q)
    acc = acc + pltpu.roll(acc, 2 * N_HEADS, axis=1)
    return acc + pltpu.roll(acc, N_HEADS, axis=1)


def _tile_slabs(x):
    return jnp.concatenate([x] * N_SLABS, axis=1)


def _prep_body(pr_ref, pk_ref, pv_ref, pl_ref, br_ref, bk_ref, bv_ref, bl_ref,
               mur_ref, muk_ref, muv_ref, mul_ref, w0_ref, a0_ref, kkp_ref, kap_ref, rkp_ref,
               wup_ref, aup_ref, gup_ref,
               r_o, w_o, k_o, v_o, kk_o, kka_o, g_o, bc_o, *, seg):
    tm = pr_ref.shape[0]
    nseg = tm // seg
    row = lax.broadcasted_iota(jnp.int32, (tm, 1), 0)

    def shifted(p_ref, b_ref, mu_ref):
        p = p_ref[...]
        prev = pltpu.roll(p, 1, axis=0)
        for s in range(nseg):
            prev = jnp.where(row == s * seg, b_ref[s:s + 1, :], prev)
        return p + mu_ref[...] * (prev - p)

    r = shifted(pr_ref, br_ref, mur_ref)
    k = shifted(pk_ref, bk_ref, muk_ref)
    v = shifted(pv_ref, bv_ref, muv_ref)
    lo = shifted(pl_ref, bl_ref, mul_ref)

    wl = jnp.tanh(lo[:, 0:LANES]).astype(BF16)
    al = lo[:, LANES:2 * LANES].astype(BF16)
    gl = _sigmoid(lo[:, 2 * LANES:LORA_PAD]).astype(BF16)
    z = w0_ref[...] + jnp.dot(wl, wup_ref[...], preferred_element_type=F32)
    decay = jnp.exp(-EXP_M05 * _sigmoid(z))
    a = _sigmoid(a0_ref[...] + jnp.dot(al, aup_ref[...], preferred_element_type=F32))
    g = jnp.dot(gl, gup_ref[...], preferred_element_type=F32)

    kk = k * kkp_ref[...]
    norm = jnp.maximum(jnp.sqrt(_head_sum(kk * kk)), 1e-12)
    kk = kk * _tile_slabs(1.0 / norm)
    k_mod = k * (1.0 + (a - 1.0) * kap_ref[...])

    r_o[...] = r
    w_o[...] = decay
    k_o[...] = k_mod
    v_o[...] = v
    kk_o[...] = kk
    kka_o[...] = kk * a
    g_o[...] = g
    bc_o[...] = _head_sum(r * k_mod * rkp_ref[...])


def _prep(p, bnd, vecs, ups, *, tm, seg):
    m = p.shape[0]
    d = D_MODEL
    lora_blk = N_GROUPS * d // LORA_PAD
    grp = lambda g: pl.BlockSpec((tm, d), lambda i, g=g: (i, g))
    bgrp = lambda g: pl.BlockSpec((SUBLANES, d), lambda i, g=g: (i, g))
    vec = pl.BlockSpec((1, d), lambda i: (0, 0))
    row_out = pl.BlockSpec((tm, d), lambda i: (i, 0))
    mur, muk, muv, mul, w0, a0, kkp, kap, rkp = vecs
    wup, aup, gup = ups
    return pl.pallas_call(
        functools.partial(_prep_body, seg=seg),
        grid=(m // tm,),
        in_specs=[grp(0), grp(1), grp(2),
                  pl.BlockSpec((tm, LORA_PAD), lambda i: (i, lora_blk)),
                  bgrp(0), bgrp(1), bgrp(2),
                  pl.BlockSpec((SUBLANES, LORA_PAD), lambda i: (i, 3 * d // LORA_PAD)),
                  vec, vec, vec, pl.BlockSpec((1, LORA_PAD), lambda i: (0, 0)),
                  vec, vec, vec, vec, vec,
                  pl.BlockSpec((LANES, d), lambda i: (0, 0)),
                  pl.BlockSpec((LANES, d), lambda i: (0, 0)),
                  pl.BlockSpec((LORA_G, d), lambda i: (0, 0))],
        out_specs=[row_out] * 7 + [pl.BlockSpec((tm, LANES), lambda i: (i, 0))],
        out_shape=[jax.ShapeDtypeStruct((m, d), F32)] * 7 + [jax.ShapeDtypeStruct((m, LANES), F32)],
        compiler_params=_cparams(("parallel",)),
        name="prep",
    )(p, p, p, p, bnd, bnd, bnd, bnd, mur, muk, muv, mul, w0, a0, kkp, kap, rkp, wup, aup, gup)


N_VT = HEAD_DIM // SUBLANES
N_KQ = N_SLABS
PAIR = 2


def _group_sum(x):
    x = x + pltpu.roll(x, 2 * N_HEADS, axis=1)
    return x + pltpu.roll(x, N_HEADS, axis=1)


def _scan_body(r_ref, w_ref, k_ref, v_ref, kk_ref, kka_ref, s0_ref, y_ref, s_ref, sa_ref, *, tt):
    @pl.when(pl.program_id(1) == 0)
    def _():
        s_ref[...] = s0_ref[...]

    sub = lax.broadcasted_iota(jnp.int32, (SUBLANES, LANES), 0)
    grp = lax.broadcasted_iota(jnp.int32, (SUBLANES, LANES), 1) // N_HEADS
    rot = (grp - sub % K_PER_SLAB) % K_PER_SLAB
    upper = sub >= K_PER_SLAB
    diag = grp == sub % K_PER_SLAB

    def tile8(ref, b, t8):
        return ref.at[b, pl.ds(pl.multiple_of(t8 * SUBLANES, SUBLANES), SUBLANES), :]

    def row(view, s, q):
        return jnp.broadcast_to(view[pl.ds(s, 1), pl.ds(q * LANES, LANES)], (SUBLANES, LANES))

    for b in range(PAIR):
        kk0 = tile8(kk_ref, b, 0)
        for j in range(N_VT):
            acc = s_ref[b, j, 0] * row(kk0, 0, 0)
            for kq in range(1, N_KQ):
                acc = acc + s_ref[b, j, kq] * row(kk0, 0, kq)
            sa_ref[b, j] = _group_sum(acc)

    def step(b, views, s, js):
        r_v, w_v, k_v, v_v, kk_v, kka_v, kkn_v, y_v = views
        vcols = {}
        for j in js:
            base = jnp.where(upper, row(v_v, s, 2 * j + 1), row(v_v, s, 2 * j))
            col = base
            for d in range(1, K_PER_SLAB):
                col = jnp.where(rot == d, pltpu.roll(base, d * N_HEADS, axis=1), col)
            vcols[j] = col
        sas = {j: sa_ref[b, j] for j in js}
        acc_y = {}
        acc_s = {}
        for kq in range(N_KQ):
            w_row = row(w_v, s, kq)
            kka_row = row(kka_v, s, kq)
            k_row = row(k_v, s, kq)
            r_row = row(r_v, s, kq)
            kkn_row = row(kk_v, s + 1, kq) if s + 1 < SUBLANES else row(kkn_v, 0, kq)
            for j in js:
                s_new = s_ref[b, j, kq] * w_row - sas[j] * kka_row + vcols[j] * k_row
                s_ref[b, j, kq] = s_new
                ty = s_new * r_row
                ts = s_new * kkn_row
                acc_y[j] = ty if kq == 0 else acc_y[j] + ty
                acc_s[j] = ts if kq == 0 else acc_s[j] + ts
        for j in js:
            sa_ref[b, j] = _group_sum(acc_s[j])
            yv = jnp.where(diag, _group_sum(acc_y[j]), 0.0)
            yv = yv + pltpu.roll(yv, 1, axis=0)
            yv = yv + pltpu.roll(yv, 2, axis=0)
            y_v[pl.ds(s, 1), pl.ds(2 * j * LANES, LANES)] = yv[K_PER_SLAB - 1:K_PER_SLAB, :]
            y_v[pl.ds(s, 1), pl.ds((2 * j + 1) * LANES, LANES)] = yv[SUBLANES - 1:SUBLANES, :]

    n8 = tt // SUBLANES

    halves = (tuple(range(N_VT // 2)), tuple(range(N_VT // 2, N_VT)))

    def steps8(t8, carry):
        t8n = jnp.minimum(t8 + 1, n8 - 1)
        views = []
        for b in range(PAIR):
            vb = tuple(tile8(ref, b, t8) for ref in (r_ref, w_ref, k_ref, v_ref, kk_ref, kka_ref))
            views.append(vb + (tile8(kk_ref, b, t8n), tile8(y_ref, b, t8)))
        for s in range(SUBLANES):
            for js in halves:
                for b in range(PAIR):
                    step(b, views[b], s, js)
        return carry

    lax.fori_loop(0, n8, steps8, 0)


def _scan(r, w, k, v, kk, kka, s0, *, tt):
    bsz, t_len, d = r.shape
    seq = pl.BlockSpec((PAIR, tt, d), lambda i, j: (i, j, 0))
    st = pl.BlockSpec((PAIR, N_VT, N_KQ, SUBLANES, LANES), lambda i, j: (i, 0, 0, 0, 0))
    return pl.pallas_call(
        functools.partial(_scan_body, tt=tt),
        grid=(bsz // PAIR, t_len // tt),
        in_specs=[seq] * 6 + [st],
        out_specs=[seq, st],
        out_shape=[jax.ShapeDtypeStruct((bsz, t_len, d), F32),
                   jax.ShapeDtypeStruct(s0.shape, F32)],
        scratch_shapes=[pltpu.VMEM((PAIR, N_VT, SUBLANES, LANES), F32)],
        compiler_params=_cparams(("parallel", "arbitrary")),
        name="wkv_scan",
    )(r, w, k, v, kk, kka, s0)


def _state_to_tiles(s):
    b = s.shape[0]
    s = s.reshape(b, N_HEADS, N_VT, SUBLANES, N_KQ, K_PER_SLAB)
    return s.transpose(0, 2, 4, 3, 5, 1).reshape(b, N_VT, N_KQ, SUBLANES, LANES)


def _tiles_to_state(s):
    b = s.shape[0]
    s = s.reshape(b, N_VT, N_KQ, SUBLANES, K_PER_SLAB, N_HEADS)
    return s.transpose(0, 5, 1, 3, 2, 4).reshape(b, N_HEADS, HEAD_DIM, HEAD_DIM)


def _post_body(y_ref, v_ref, bc_ref, g_ref, gtb_ref, gtc_ref, xc_ref, ga_ref, gb_ref, x1_ref,
               bgc_ref, bxc_ref, lng_ref, lnb_ref, cw_ref, wout_ref, x2_o, ut_o, *, seg):
    tm = y_ref.shape[0]
    nseg = tm // seg
    row = lax.broadcasted_iota(jnp.int32, (tm, 1), 0)
    inv_n = 1.0 / HEAD_DIM

    o = y_ref[...]
    mu = _head_sum(o) * inv_n
    c = o - _tile_slabs(mu)
    var = _head_sum(c * c) * inv_n
    o = c * _tile_slabs(lax.rsqrt(var + GN_EPS)) * lng_ref[...] + lnb_ref[...]
    bonus = _tile_slabs(bc_ref[...]) * v_ref[...]
    y_a = (o + bonus) * g_ref[...]

    u = gtc_ref[...] * xc_ref[...]
    ub = bgc_ref[...] * bxc_ref[...]
    u1 = pltpu.roll(u, 1, axis=0)
    u2 = pltpu.roll(u, 2, axis=0)
    for s in range(nseg):
        u1 = jnp.where(row == s * seg, ub[2 * s + 1:2 * s + 2, :], u1)
        u2 = jnp.where(row == s * seg, ub[2 * s:2 * s + 1, :], u2)
        u2 = jnp.where(row == s * seg + 1, ub[2 * s + 1:2 * s + 2, :], u2)
    conv = cw_ref[0:1, :] * u2 + cw_ref[1:2, :] * u1 + cw_ref[2:3, :] * u
    y_b = gtb_ref[...] * conv

    merged = _sigmoid(ga_ref[...]) * y_a + _sigmoid(gb_ref[...]) * y_b
    x2_o[...] = x1_ref[...] + jnp.dot(merged.astype(BF16), wout_ref[...], preferred_element_type=F32)

    ut_o[...] = jnp.zeros_like(ut_o)
    for s in range(nseg):
        ut_o[2 * s:2 * s + 2, :] = u[(s + 1) * seg - 2:(s + 1) * seg, :]


def _post(y, v, bc, g, p, x1, bgc, bxc, lng, lnb, cw, wout, *, tm, seg):
    m, d = y.shape
    rowb = pl.BlockSpec((tm, d), lambda i: (i, 0))
    grp = lambda gi: pl.BlockSpec((tm, d), lambda i, gi=gi: (i, gi))
    vec = pl.BlockSpec((1, d), lambda i: (0, 0))
    bnd = pl.BlockSpec((SUBLANES, d), lambda i: (i, 0))
    return pl.pallas_call(
        functools.partial(_post_body, seg=seg),
        grid=(m // tm,),
        in_specs=[rowb, rowb, pl.BlockSpec((tm, LANES), lambda i: (i, 0)), rowb,
                  grp(3), grp(4), grp(5), grp(6), grp(7), rowb,
                  bnd, bnd, vec, vec,
                  pl.BlockSpec((CONV_W, d), lambda i: (0, 0)),
                  pl.BlockSpec((d, d), lambda i: (0, 0), pipeline_mode=pl.Buffered(1))],
        out_specs=[rowb, bnd],
        out_shape=[jax.ShapeDtypeStruct((m, d), F32),
                   jax.ShapeDtypeStruct((m // tm * SUBLANES, d), F32)],
        compiler_params=_cparams(("parallel",)),
        name="post",
    )(y, v, bc, g, p, p, p, p, p, x1, bgc, bxc, lng, lnb, cw, wout)


def _hm(x):
    lead = x.shape[:-1]
    return x.reshape(*lead, N_HEADS, HEAD_DIM).swapaxes(-1, -2).reshape(*lead, D_MODEL)


def _hm_inv(x):
    lead = x.shape[:-1]
    return x.reshape(*lead, HEAD_DIM, N_HEADS).swapaxes(-1, -2).reshape(*lead, D_MODEL)


def _pad_lora(x):
    z = jnp.zeros(x.shape[:-1] + (LANES - LORA_W,), x.dtype)
    return jnp.concatenate([x[..., :LORA_W], z, x[..., LORA_W:LORA_W + LORA_A], z,
                            x[..., LORA_W + LORA_A:]], axis=-1)


def _pad_rows(x, rows):
    return jnp.concatenate([x, jnp.zeros((rows - x.shape[0],) + x.shape[1:], x.dtype)], axis=0)


def _tile_bounds(rows, n_tiles, n_used):
    w = rows.shape[-1]
    rows = rows.reshape(n_tiles, n_used, w)
    pad = jnp.zeros((n_tiles, SUBLANES - n_used, w), rows.dtype)
    return jnp.concatenate([rows, pad], axis=1).reshape(n_tiles * SUBLANES, w)


def _layer(x, shift_prev, conv_prev, wkv_prev, wts, *, tm_prep, tm_post, seg_prep, seg_post, tt):
    bsz, t_len, d = x.shape
    m = bsz * t_len
    xf = x.reshape(m, d)

    x1, h = _ffn(xf, wts["ffn1_norm"], wts["ffn1_gate"], wts["ffn1_up"], wts["ffn1_down"],
                 wts["mix_norm"], emit_x=True)
    p = _proj(h, wts["w_groups"], tm=min(m, 1024))

    n_rkv = 3 * d
    n_tiles = m // tm_prep
    if shift_prev is None:
        starts = jnp.arange(n_tiles) * tm_prep
        prev = jnp.take(p, jnp.maximum(starts - 1, 0), axis=0)
        prev = jnp.where((starts % t_len == 0)[:, None], 0.0, prev)
        bnd = jnp.concatenate([prev[:, :n_rkv], prev[:, N_GROUPS * d:]], axis=1)
        n_used = 1
    else:
        rows = shift_prev.shape[0]
        ws = wts["w_groups"]
        bnd = _proj(_pad_rows(shift_prev, 32), ws[:3] + ws[N_GROUPS:], tm=32)[:rows]
        n_used = tm_prep // seg_prep
    bnd = _tile_bounds(bnd, n_tiles, n_used)

    r, w, k, v, kk, kka, g, bc = _prep(p, bnd, wts["prep_vecs"], wts["prep_ups"],
                                       tm=tm_prep, seg=seg_prep)

    s0 = _state_to_tiles(wkv_prev)
    sh = (bsz, t_len, d)
    y, s_new = _scan(r.reshape(sh), w.reshape(sh), k.reshape(sh), v.reshape(sh),
                     kk.reshape(sh), kka.reshape(sh), s0, tt=tt)
    wkv_new = _tiles_to_state(s_new)

    n_tiles = m // tm_post
    if conv_prev is None:
        starts = jnp.arange(n_tiles) * tm_post
        idx = jnp.maximum(starts[:, None] + jnp.array([-2, -1])[None, :], 0).reshape(-1)
        rows = jnp.take(p, idx, axis=0)
        live = jnp.repeat(starts % t_len != 0, 2)[:, None]
        bgc = jnp.where(live, rows[:, 4 * d:5 * d], 0.0)
        bxc = rows[:, 5 * d:6 * d]
        n_used = 2
    else:
        bgc = _hm(conv_prev).reshape(bsz * (CONV_W - 1), d)
        bxc = jnp.ones_like(bgc)
        n_used = 2 * (tm_post // seg_post)
    bgc = _tile_bounds(bgc, n_tiles, n_used)
    bxc = _tile_bounds(bxc, n_tiles, n_used)

    x2, utail = _post(y.reshape(m, d), v, bc, g, p, x1, bgc, bxc,
                      wts["ln_g"], wts["ln_b"], wts["conv_w"], wts["w_out"],
                      tm=tm_post, seg=seg_post)

    (yout,) = _ffn(x2, wts["ffn2_norm"], wts["ffn2_gate"], wts["ffn2_up"], wts["ffn2_down"],
                   wts["final_norm"], emit_x=False)

    shift_new = h.reshape(bsz, t_len, d)[:, -1]
    nseg_post = tm_post // seg_post
    ut = utail.reshape(n_tiles, SUBLANES, d)[:, :2 * nseg_post]
    ut = ut.reshape(n_tiles * nseg_post, 2, d)
    segs_per_seq = t_len // seg_post
    conv_new = _hm_inv(ut[segs_per_seq - 1::segs_per_seq])
    return yout.reshape(bsz, t_len, d), shift_new, conv_new, wkv_new


def kernel(x_prompt, x_sample, state_shift, state_conv, state_wkv, ffn1_norm, ffn1_gate, ffn1_up, ffn1_down, mix_norm, w_in, mu_shift, w0, w_lora_up, a0, a_lora_up, g_lora_up, k_k, k_a, r_k, ln_x_g, ln_x_b, conv_w, w_out, ffn2_norm, ffn2_gate, ffn2_up, ffn2_down, final_norm):
    depth = w_in.shape[0]
    assert depth == 1, "single-layer kernel"
    d = D_MODEL
    n_a = 3 * d + LORA_W + LORA_A + LORA_G
    l = 0

    wi = w_in[l]
    groups = [wi[:, 0:d], wi[:, d:2 * d], wi[:, 2 * d:3 * d]]
    groups += [wi[:, n_a + i * d:n_a + (i + 1) * d] for i in range(5)]
    w_groups = [_hm(gw).astype(BF16) for gw in groups] + [_pad_lora(wi[:, 3 * d:n_a]).astype(BF16)]

    mu = mu_shift[l]
    row = lambda x: x.reshape(1, -1)
    prep_vecs = (row(_hm(mu[0:d])), row(_hm(mu[d:2 * d])), row(_hm(mu[2 * d:3 * d])),
                 row(_pad_lora(mu[3 * d:])),
                 row(_hm(w0[l])), row(_hm(a0[l])), row(_hm(k_k[l])), row(_hm(k_a[l])),
                 row(_hm(r_k[l].reshape(d))))
    prep_ups = (_pad_rows(_hm(w_lora_up[l]), LANES).astype(BF16),
                _pad_rows(_hm(a_lora_up[l]), LANES).astype(BF16),
                _hm(g_lora_up[l]).astype(BF16))

    wts = dict(
        ffn1_norm=row(ffn1_norm[l]), ffn1_gate=ffn1_gate[l].astype(BF16), ffn1_up=ffn1_up[l].astype(BF16),
        ffn1_down=ffn1_down[l].astype(BF16), mix_norm=row(mix_norm[l]), w_groups=w_groups,
        prep_vecs=prep_vecs, prep_ups=prep_ups,
        ln_g=row(_hm(ln_x_g[l])), ln_b=row(_hm(ln_x_b[l])), conv_w=_hm(conv_w[l]),
        w_out=_hm(w_out[l].T).T.astype(BF16),
        ffn2_norm=row(ffn2_norm[l]), ffn2_gate=ffn2_gate[l].astype(BF16), ffn2_up=ffn2_up[l].astype(BF16),
        ffn2_down=ffn2_down[l].astype(BF16), final_norm=row(final_norm),
    )

    b_p, t_p, _ = x_prompt.shape
    b_s, t_s, _ = x_sample.shape
    zero_state = jnp.zeros((b_p, N_HEADS, HEAD_DIM, HEAD_DIM), state_wkv.dtype)
    y_p, sh_p, cv_p, wk_p = _layer(x_prompt, None, None, zero_state, wts,
                                   tm_prep=128, tm_post=256, seg_prep=128, seg_post=256, tt=64)
    y_s, sh_s, cv_s, wk_s = _layer(x_sample, state_shift[l], state_conv[l], state_wkv[l], wts,
                                   tm_prep=128, tm_post=256, seg_prep=t_s, seg_post=t_s, tt=t_s)
    return (y_p, y_s, sh_p[None], cv_p[None], wk_p[None], sh_s[None], cv_s[None], wk_s[None])
```

```python
import functools
import math

import jax
import jax.numpy as jnp
from jax import lax
from jax.experimental import pallas as pl
from jax.experimental.pallas import tpu as pltpu

F32 = jnp.float32
BF16 = jnp.bfloat16

HEAD_DIM = 64
N_HEADS = 32
D_MODEL = N_HEADS * HEAD_DIM
CONV_W = 3
LORA_W = 96
LORA_A = 96
LORA_G = 256
RMS_EPS = 1e-6
GN_EPS = 64e-5
EXP_M05 = math.exp(-0.5)

LANES = 128
SUBLANES = 8
N_SLABS = D_MODEL // LANES
K_PER_SLAB = LANES // N_HEADS
LORA_PAD = 512
N_GROUPS = 8
N_PROJ = N_GROUPS * D_MODEL + LORA_PAD
VMEM_LIMIT = 56 * 1024 * 1024
FFN_TM = 512


def _cparams(sem):
    return pltpu.CompilerParams(dimension_semantics=sem, vmem_limit_bytes=VMEM_LIMIT)


def _sigmoid(x):
    return 1.0 / (1.0 + jnp.exp(-x))


def _ffn_body(x_ref, g1_ref, wg_ref, wu_ref, wd_ref, g2_ref, *rest, seg):
    if seg is None:
        ho_ref, xn_ref, acc_ref = rest
    else:
        xo_ref, ho_ref, hl_ref, xn_ref, acc_ref = rest
    j = pl.program_id(1)

    @pl.when(j == 0)
    def _():
        x = x_ref[...]
        ms = jnp.mean(x * x, axis=-1, keepdims=True)
        xn_ref[...] = (x * lax.rsqrt(ms + RMS_EPS) * g1_ref[...]).astype(BF16)
        acc_ref[...] = jnp.zeros_like(acc_ref)

    xn = xn_ref[...]
    gate = jnp.dot(xn, wg_ref[...], preferred_element_type=F32)
    up = jnp.dot(xn, wu_ref[...], preferred_element_type=F32)
    act = (gate * _sigmoid(gate) * up).astype(BF16)
    acc_ref[...] += jnp.dot(act, wd_ref[...], preferred_element_type=F32)

    @pl.when(j == pl.num_programs(1) - 1)
    def _():
        x1 = x_ref[...] + 0.5 * acc_ref[...]
        ms = jnp.mean(x1 * x1, axis=-1, keepdims=True)
        h = x1 * lax.rsqrt(ms + RMS_EPS) * g2_ref[...]
        if seg is None:
            ho_ref[...] = h
        else:
            xo_ref[...] = x1
            ho_ref[...] = h.astype(BF16)
            hl_ref[...] = jnp.zeros_like(hl_ref)
            for s in range(x1.shape[0] // seg):
                hl_ref[s:s + 1, :] = h[(s + 1) * seg - 1:(s + 1) * seg, :]


def _ffn(x, g1, wg, wu, wd, g2, *, seg, tm=FFN_TM, tf=512):
    m, d = x.shape
    f = wg.shape[1]
    grid = (m // tm, f // tf)
    row = pl.BlockSpec((tm, d), lambda i, j: (i, 0))
    vec = pl.BlockSpec((1, d), lambda i, j: (0, 0))
    if seg is None:
        out_specs = [row]
        out_shape = [jax.ShapeDtypeStruct((m, d), F32)]
    else:
        assert tm // seg <= SUBLANES
        out_specs = [row, row, pl.BlockSpec((SUBLANES, d), lambda i, j: (i, 0))]
        out_shape = [jax.ShapeDtypeStruct((m, d), F32), jax.ShapeDtypeStruct((m, d), BF16),
                     jax.ShapeDtypeStruct((m // tm * SUBLANES, d), F32)]
    return pl.pallas_call(
        functools.partial(_ffn_body, seg=seg),
        grid=grid,
        in_specs=[row, vec,
                  pl.BlockSpec((d, tf), lambda i, j: (0, j)),
                  pl.BlockSpec((d, tf), lambda i, j: (0, j)),
                  pl.BlockSpec((tf, d), lambda i, j: (j, 0)),
                  vec],
        out_specs=out_specs,
        out_shape=out_shape,
        scratch_shapes=[pltpu.VMEM((tm, d), BF16), pltpu.VMEM((tm, d), F32)],
        compiler_params=_cparams(("parallel", "arbitrary")),
        name="ffn",
    )(x, g1, wg, wu, wd, g2)


def _proj_body(h_ref, *refs, starts):
    w_refs, o_ref = refs[:-1], refs[-1]
    j = pl.program_id(1)
    for g, w_ref in enumerate(w_refs):
        @pl.when((j >= starts[g]) & (j < starts[g + 1]))
        def _(w_ref=w_ref):
            o_ref[...] = jnp.dot(h_ref[...], w_ref[...], preferred_element_type=F32)


def _proj(h, ws, *, tm, tn=512):
    m, d = h.shape
    starts = [0]
    for w in ws:
        starts.append(starts[-1] + w.shape[1] // tn)
    w_specs = [pl.BlockSpec((d, tn), lambda i, j, lo=lo, hi=hi: (0, jnp.clip(j - lo, 0, hi - lo - 1)))
               for lo, hi in zip(starts[:-1], starts[1:])]
    return pl.pallas_call(
        functools.partial(_proj_body, starts=tuple(starts)),
        grid=(m // tm, starts[-1]),
        in_specs=[pl.BlockSpec((tm, d), lambda i, j: (i, 0))] + w_specs,
        out_specs=pl.BlockSpec((tm, tn), lambda i, j: (i, j)),
        out_shape=jax.ShapeDtypeStruct((m, starts[-1] * tn), F32),
        compiler_params=_cparams(("parallel", "arbitrary")),
        name="proj",
    )(h, *ws)


def _slab(x, q):
    return x[:, q * LANES:(q + 1) * LANES]


def _head_sum(x):
    acc = _slab(x, 0)
    for q in range(1, N_SLABS):
        acc = acc + _slab(x, q)
    acc = acc + pltpu.roll(acc, 2 * N_HEADS, axis=1)
    return acc + pltpu.roll(acc, N_HEADS, axis=1)


def _tile_slabs(x):
    return jnp.concatenate([x] * N_SLABS, axis=1)


def _prep_body(pr_ref, pk_ref, pv_ref, pl_ref, br_ref, bk_ref, bv_ref, bl_ref,
               mur_ref, muk_ref, muv_ref, mul_ref, w0_ref, a0_ref, kkp_ref, kap_ref, rkp_ref,
               wup_ref, aup_ref, gup_ref,
               r_o, w_o, k_o, v_o, kk_o, kka_o, g_o, bc_o, *, seg):
    tm = pr_ref.shape[0]
    nseg = tm // seg
    row = lax.broadcasted_iota(jnp.int32, (tm, 1), 0)

    def shifted(p_ref, b_ref, mu_ref):
        p = p_ref[...]
        prev = pltpu.roll(p, 1, axis=0)
        for s in range(nseg):
            prev = jnp.where(row == s * seg, b_ref[s:s + 1, :], prev)
        return p + mu_ref[...] * (prev - p)

    r = shifted(pr_ref, br_ref, mur_ref)
    k = shifted(pk_ref, bk_ref, muk_ref)
    v = shifted(pv_ref, bv_ref, muv_ref)
    lo = shifted(pl_ref, bl_ref, mul_ref)

    wl = jnp.tanh(lo[:, 0:LANES]).astype(BF16)
    al = lo[:, LANES:2 * LANES].astype(BF16)
    gl = _sigmoid(lo[:, 2 * LANES:LORA_PAD]).astype(BF16)
    z = w0_ref[...] + jnp.dot(wl, wup_ref[...], preferred_element_type=F32)
    decay = jnp.exp(-EXP_M05 * _sigmoid(z))
    a = _sigmoid(a0_ref[...] + jnp.dot(al, aup_ref[...], preferred_element_type=F32))
    g = jnp.dot(gl, gup_ref[...], preferred_element_type=F32)

    kk = k * kkp_ref[...]
    norm = jnp.maximum(jnp.sqrt(_head_sum(kk * kk)), 1e-12)
    kk = kk * _tile_slabs(1.0 / norm)
    k_mod = k * (1.0 + (a - 1.0) * kap_ref[...])

    pos = row % SUBLANES
    q = decay
    for sh in (1, 2, 4):
        q = q * jnp.where(pos >= sh, pltpu.roll(q, sh, axis=0), 1.0)
    inv_q = 1.0 / q

    r_o[...] = r * q
    w_o[...] = q
    k_o[...] = k_mod * inv_q
    v_o[...] = v
    kk_o[...] = kk
    kka_o[...] = kk * a * inv_q
    g_o[...] = g
    bc_o[...] = _head_sum(r * k_mod * rkp_ref[...])


def _prep(p, bnd, vecs, ups, *, tm, seg):
    m = p.shape[0]
    d = D_MODEL
    lora_blk = N_GROUPS * d // LORA_PAD
    grp = lambda g: pl.BlockSpec((tm, d), lambda i, g=g: (i, g))
    bgrp = lambda g: pl.BlockSpec((SUBLANES, d), lambda i, g=g: (i, g))
    vec = pl.BlockSpec((1, d), lambda i: (0, 0))
    row_out = pl.BlockSpec((tm, d), lambda i: (i, 0))
    mur, muk, muv, mul, w0, a0, kkp, kap, rkp = vecs
    wup, aup, gup = ups
    return pl.pallas_call(
        functools.partial(_prep_body, seg=seg),
        grid=(m // tm,),
        in_specs=[grp(0), grp(1), grp(2),
                  pl.BlockSpec((tm, LORA_PAD), lambda i: (i, lora_blk)),
                  bgrp(0), bgrp(1), bgrp(2),
                  pl.BlockSpec((SUBLANES, LORA_PAD), lambda i: (i, 3 * d // LORA_PAD)),
                  vec, vec, vec, pl.BlockSpec((1, LORA_PAD), lambda i: (0, 0)),
                  vec, vec, vec, vec, vec,
                  pl.BlockSpec((LANES, d), lambda i: (0, 0)),
                  pl.BlockSpec((LANES, d), lambda i: (0, 0)),
                  pl.BlockSpec((LORA_G, d), lambda i: (0, 0))],
        out_specs=[row_out] * 7 + [pl.BlockSpec((tm, LANES), lambda i: (i, 0))],
        out_shape=[jax.ShapeDtypeStruct((m, d), F32)] * 7 + [jax.ShapeDtypeStruct((m, LANES), F32)],
        compiler_params=_cparams(("parallel",)),
        name="prep",
    )(p, p, p, p, bnd, bnd, bnd, bnd, mur, muk, muv, mul, w0, a0, kkp, kap, rkp, wup, aup, gup)


N_VT = HEAD_DIM // SUBLANES
N_KQ = N_SLABS
PAIR = 2


def _group_sum(x):
    x = x + pltpu.roll(x, 2 * N_HEADS, axis=1)
    return x + pltpu.roll(x, N_HEADS, axis=1)


def _scan_body(r_ref, w_ref, k_ref, v_ref, kk_ref, kka_ref, kkh_ref, s0_ref, y_ref, s_ref, sa_ref, *, tt):
    sub = lax.broadcasted_iota(jnp.int32, (SUBLANES, LANES), 0)
    grp = lax.broadcasted_iota(jnp.int32, (SUBLANES, LANES), 1) // N_HEADS
    rot = (grp - sub % K_PER_SLAB) % K_PER_SLAB
    upper = sub >= K_PER_SLAB
    diag = grp == sub % K_PER_SLAB

    def tile8(ref, b, t8):
        start = t8 * SUBLANES
        if not isinstance(t8, int):
            start = pl.multiple_of(start, SUBLANES)
        return ref.at[b, pl.ds(start, SUBLANES), :]

    def row(view, s, q):
        return jnp.broadcast_to(view[pl.ds(s, 1), pl.ds(q * LANES, LANES)], (SUBLANES, LANES))

    @pl.when(pl.program_id(1) == 0)
    def _():
        s_ref[...] = s0_ref[...]
        for b in range(PAIR):
            kk0 = tile8(kk_ref, b, 0)
            for j in range(N_VT):
                acc = s0_ref[b, j, 0] * row(kk0, 0, 0)
                for kq in range(1, N_KQ):
                    acc = acc + s0_ref[b, j, kq] * row(kk0, 0, kq)
                sa_ref[b, j] = _group_sum(acc)

    def step(b, views, s, js):
        r_v, w_v, k_v, v_v, kk_v, kka_v, kkn_v, y_v = views
        vcols = {}
        for j in js:
            base = jnp.where(upper, row(v_v, s, 2 * j + 1), row(v_v, s, 2 * j))
            col = base
            for d in range(1, K_PER_SLAB):
                col = jnp.where(rot == d, pltpu.roll(base, d * N_HEADS, axis=1), col)
            vcols[j] = col
        sas = {j: sa_ref[b, j] for j in js}
        acc_y = {}
        acc_s = {}
        for kq in range(N_KQ):
            q_row = row(w_v, s, kq)
            kka_row = row(kka_v, s, kq)
            k_row = row(k_v, s, kq)
            r_row = row(r_v, s, kq)
            kkn_row = row(kk_v, s + 1, kq) if s + 1 < SUBLANES else row(kkn_v, 0, kq)
            qkn_row = q_row * kkn_row
            for j in js:
                s_new = s_ref[b, j, kq] - sas[j] * kka_row + vcols[j] * k_row
                s_ref[b, j, kq] = s_new * q_row if s == SUBLANES - 1 else s_new
                ty = s_new * r_row
                ts = s_new * qkn_row
                acc_y[j] = ty if kq == 0 else acc_y[j] + ty
                acc_s[j] = ts if kq == 0 else acc_s[j] + ts
        for j in js:
            sa_ref[b, j] = _group_sum(acc_s[j])
            yv = jnp.where(diag, _group_sum(acc_y[j]), 0.0)
            yv = yv + pltpu.roll(yv, 1, axis=0)
            yv = yv + pltpu.roll(yv, 2, axis=0)
            y_v[pl.ds(s, 1), pl.ds(2 * j * LANES, LANES)] = yv[K_PER_SLAB - 1:K_PER_SLAB, :]
            y_v[pl.ds(s, 1), pl.ds((2 * j + 1) * LANES, LANES)] = yv[SUBLANES - 1:SUBLANES, :]

    n8 = tt // SUBLANES

    halves = (tuple(range(N_VT // 2)), tuple(range(N_VT // 2, N_VT)))

    def steps8(t8, last):
        views = []
        for b in range(PAIR):
            vb = tuple(tile8(ref, b, t8) for ref in (r_ref, w_ref, k_ref, v_ref, kk_ref, kka_ref))
            nxt = kkh_ref.at[b] if last else tile8(kk_ref, b, t8 + 1)
            views.append(vb + (nxt, tile8(y_ref, b, t8)))
        for s in range(SUBLANES):
            for js in halves:
                for b in range(PAIR):
                    step(b, views[b], s, js)

    def loop_body(t8, carry):
        steps8(t8, False)
        return carry

    lax.fori_loop(0, n8 - 1, loop_body, 0)
    steps8(n8 - 1, True)


def _scan(r, w, k, v, kk, kka, s0, *, tt):
    bsz, t_len, d = r.shape
    seq = pl.BlockSpec((PAIR, tt, d), lambda i, j: (i, j, 0))
    st = pl.BlockSpec((PAIR, N_VT, N_KQ, SUBLANES, LANES), lambda i, j: (i, 0, 0, 0, 0))
    n8 = tt // SUBLANES
    last8 = t_len // SUBLANES - 1
    head = pl.BlockSpec((PAIR, SUBLANES, d), lambda i, j: (i, jnp.minimum((j + 1) * n8, last8), 0))
    return pl.pallas_call(
        functools.partial(_scan_body, tt=tt),
        grid=(bsz // PAIR, t_len // tt),
        in_specs=[seq] * 6 + [head, st],
        out_specs=[seq, st],
        out_shape=[jax.ShapeDtypeStruct((bsz, t_len, d), F32),
                   jax.ShapeDtypeStruct(s0.shape, F32)],
        scratch_shapes=[pltpu.VMEM((PAIR, N_VT, SUBLANES, LANES), F32)],
        compiler_params=_cparams(("parallel", "arbitrary")),
        name="wkv_scan",
    )(r, w, k, v, kk, kka, kk, s0)


def _state_to_tiles(s):
    b = s.shape[0]
    s = s.reshape(b, N_HEADS, N_VT, SUBLANES, N_KQ, K_PER_SLAB)
    return s.transpose(0, 2, 4, 3, 5, 1).reshape(b, N_VT, N_KQ, SUBLANES, LANES)


def _tiles_to_state(s):
    b = s.shape[0]
    s = s.reshape(b, N_VT, N_KQ, SUBLANES, K_PER_SLAB, N_HEADS)
    return s.transpose(0, 5, 1, 3, 2, 4).reshape(b, N_HEADS, HEAD_DIM, HEAD_DIM)


def _post_body(y_ref, v_ref, bc_ref, g_ref, gtb_ref, gtc_ref, xc_ref, ga_ref, gb_ref, x1_ref,
               bgc_ref, bxc_ref, lng_ref, lnb_ref, cw_ref, wout_ref, x2_o, ut_o, *, seg):
    tm = y_ref.shape[0]
    nseg = tm // seg
    row = lax.broadcasted_iota(jnp.int32, (tm, 1), 0)
    inv_n = 1.0 / HEAD_DIM

    o = y_ref[...]
    mu = _head_sum(o) * inv_n
    c = o - _tile_slabs(mu)
    var = _head_sum(c * c) * inv_n
    o = c * _tile_slabs(lax.rsqrt(var + GN_EPS)) * lng_ref[...] + lnb_ref[...]
    bonus = _tile_slabs(bc_ref[...]) * v_ref[...]
    y_a = (o + bonus) * g_ref[...]

    u = gtc_ref[...] * xc_ref[...]
    ub = bgc_ref[...] * bxc_ref[...]
    u1 = pltpu.roll(u, 1, axis=0)
    u2 = pltpu.roll(u, 2, axis=0)
    for s in range(nseg):
        u1 = jnp.where(row == s * seg, ub[2 * s + 1:2 * s + 2, :], u1)
        u2 = jnp.where(row == s * seg, ub[2 * s:2 * s + 1, :], u2)
        u2 = jnp.where(row == s * seg + 1, ub[2 * s + 1:2 * s + 2, :], u2)
    conv = cw_ref[0:1, :] * u2 + cw_ref[1:2, :] * u1 + cw_ref[2:3, :] * u
    y_b = gtb_ref[...] * conv

    merged = _sigmoid(ga_ref[...]) * y_a + _sigmoid(gb_ref[...]) * y_b
    x2_o[...] = x1_ref[...] + jnp.dot(merged.astype(BF16), wout_ref[...], preferred_element_type=F32)

    ut_o[...] = jnp.zeros_like(ut_o)
    for s in range(nseg):
        ut_o[2 * s:2 * s + 2, :] = u[(s + 1) * seg - 2:(s + 1) * seg, :]


def _post(y, v, bc, g, p, x1, bgc, bxc, lng, lnb, cw, wout, *, tm, seg):
    m, d = y.shape
    rowb = pl.BlockSpec((tm, d), lambda i: (i, 0))
    grp = lambda gi: pl.BlockSpec((tm, d), lambda i, gi=gi: (i, gi))
    vec = pl.BlockSpec((1, d), lambda i: (0, 0))
    bnd = pl.BlockSpec((SUBLANES, d), lambda i: (i, 0))
    return pl.pallas_call(
        functools.partial(_post_body, seg=seg),
        grid=(m // tm,),
        in_specs=[rowb, rowb, pl.BlockSpec((tm, LANES), lambda i: (i, 0)), rowb,
                  grp(3), grp(4), grp(5), grp(6), grp(7), rowb,
                  bnd, bnd, vec, vec,
                  pl.BlockSpec((CONV_W, d), lambda i: (0, 0)),
                  pl.BlockSpec((d, d), lambda i: (0, 0), pipeline_mode=pl.Buffered(1))],
        out_specs=[rowb, bnd],
        out_shape=[jax.ShapeDtypeStruct((m, d), F32),
                   jax.ShapeDtypeStruct((m // tm * SUBLANES, d), F32)],
        compiler_params=_cparams(("parallel",)),
        name="post",
    )(y, v, bc, g, p, p, p, p, p, x1, bgc, bxc, lng, lnb, cw, wout)


def _hm(x):
    lead = x.shape[:-1]
    return x.reshape(*lead, N_HEADS, HEAD_DIM).swapaxes(-1, -2).reshape(*lead, D_MODEL)


def _hm_inv(x):
    lead = x.shape[:-1]
    return x.reshape(*lead, HEAD_DIM, N_HEADS).swapaxes(-1, -2).reshape(*lead, D_MODEL)


def _pad_lora(x):
    z = jnp.zeros(x.shape[:-1] + (LANES - LORA_W,), x.dtype)
    return jnp.concatenate([x[..., :LORA_W], z, x[..., LORA_W:LORA_W + LORA_A], z,
                            x[..., LORA_W + LORA_A:]], axis=-1)


def _pad_rows(x, rows):
    return jnp.concatenate([x, jnp.zeros((rows - x.shape[0],) + x.shape[1:], x.dtype)], axis=0)


def _tile_bounds(rows, n_tiles, n_used):
    w = rows.shape[-1]
    rows = rows.reshape(n_tiles, n_used, w)
    pad = jnp.zeros((n_tiles, SUBLANES - n_used, w), rows.dtype)
    return jnp.concatenate([rows, pad], axis=1).reshape(n_tiles * SUBLANES, w)


def _layer(x, shift_prev, conv_prev, wkv_prev, wts, *, tm_prep, tm_post, seg_prep, seg_post, tt):
    bsz, t_len, d = x.shape
    m = bsz * t_len
    xf = x.reshape(m, d)

    x1, h, h_last = _ffn(xf, wts["ffn1_norm"], wts["ffn1_gate"], wts["ffn1_up"], wts["ffn1_down"],
                         wts["mix_norm"], seg=min(t_len, FFN_TM))
    p = _proj(h, wts["w_groups"], tm=min(m, 1024))

    n_rkv = 3 * d
    n_tiles = m // tm_prep
    if shift_prev is None:
        starts = jnp.arange(n_tiles) * tm_prep
        prev = jnp.take(p, jnp.maximum(starts - 1, 0), axis=0)
        prev = jnp.where((starts % t_len == 0)[:, None], 0.0, prev)
        bnd = jnp.concatenate([prev[:, :n_rkv], prev[:, N_GROUPS * d:]], axis=1)
        n_used = 1
    else:
        rows = shift_prev.shape[0]
        ws = wts["w_groups"]
        bnd = _proj(_pad_rows(shift_prev, 32).astype(BF16), ws[:3] + ws[N_GROUPS:], tm=32)[:rows]
        n_used = tm_prep // seg_prep
    bnd = _tile_bounds(bnd, n_tiles, n_used)

    r, w, k, v, kk, kka, g, bc = _prep(p, bnd, wts["prep_vecs"], wts["prep_ups"],
                                       tm=tm_prep, seg=seg_prep)

    s0 = _state_to_tiles(wkv_prev)
    sh = (bsz, t_len, d)
    y, s_new = _scan(r.reshape(sh), w.reshape(sh), k.reshape(sh), v.reshape(sh),
                     kk.reshape(sh), kka.reshape(sh), s0, tt=tt)
    wkv_new = _tiles_to_state(s_new)

    n_tiles = m // tm_post
    if conv_prev is None:
        starts = jnp.arange(n_tiles) * tm_post
        idx = jnp.maximum(starts[:, None] + jnp.array([-2, -1])[None, :], 0).reshape(-1)
        rows = jnp.take(p, idx, axis=0)
        live = jnp.repeat(starts % t_len != 0, 2)[:, None]
        bgc = jnp.where(live, rows[:, 4 * d:5 * d], 0.0)
        bxc = rows[:, 5 * d:6 * d]
        n_used = 2
    else:
        bgc = _hm(conv_prev).reshape(bsz * (CONV_W - 1), d)
        bxc = jnp.ones_like(bgc)
        n_used = 2 * (tm_post // seg_post)
    bgc = _tile_bounds(bgc, n_tiles, n_used)
    bxc = _tile_bounds(bxc, n_tiles, n_used)

    x2, utail = _post(y.reshape(m, d), v, bc, g, p, x1, bgc, bxc,
                      wts["ln_g"], wts["ln_b"], wts["conv_w"], wts["w_out"],
                      tm=tm_post, seg=seg_post)

    (yout,) = _ffn(x2, wts["ffn2_norm"], wts["ffn2_gate"], wts["ffn2_up"], wts["ffn2_down"],
                   wts["final_norm"], seg=None)

    segs = m // min(t_len, FFN_TM)
    h_last = h_last.reshape(m // FFN_TM, SUBLANES, d)[:, :segs // (m // FFN_TM)].reshape(segs, d)
    shift_new = h_last.reshape(bsz, segs // bsz, d)[:, -1]
    nseg_post = tm_post // seg_post
    ut = utail.reshape(n_tiles, SUBLANES, d)[:, :2 * nseg_post]
    ut = ut.reshape(n_tiles * nseg_post, 2, d)
    segs_per_seq = t_len // seg_post
    conv_new = _hm_inv(ut[segs_per_seq - 1::segs_per_seq])
    return yout.reshape(bsz, t_len, d), shift_new, conv_new, wkv_new


def kernel(x_prompt, x_sample, state_shift, state_conv, state_wkv, ffn1_norm, ffn1_gate, ffn1_up, ffn1_down, mix_norm, w_in, mu_shift, w0, w_lora_up, a0, a_lora_up, g_lora_up, k_k, k_a, r_k, ln_x_g, ln_x_b, conv_w, w_out, ffn2_norm, ffn2_gate, ffn2_up, ffn2_down, final_norm):
    depth = w_in.shape[0]
    assert depth == 1, "single-layer kernel"
    d = D_MODEL
    n_a = 3 * d + LORA_W + LORA_A + LORA_G
    l = 0

    wi = w_in[l]
    groups = [wi[:, 0:d], wi[:, d:2 * d], wi[:, 2 * d:3 * d]]
    groups += [wi[:, n_a + i * d:n_a + (i + 1) * d] for i in range(5)]
    w_groups = [_hm(gw).astype(BF16) for gw in groups] + [_pad_lora(wi[:, 3 * d:n_a]).astype(BF16)]

    mu = mu_shift[l]
    row = lambda x: x.reshape(1, -1)
    prep_vecs = (row(_hm(mu[0:d])), row(_hm(mu[d:2 * d])), row(_hm(mu[2 * d:3 * d])),
                 row(_pad_lora(mu[3 * d:])),
                 row(_hm(w0[l])), row(_hm(a0[l])), row(_hm(k_k[l])), row(_hm(k_a[l])),
                 row(_hm(r_k[l].reshape(d))))
    prep_ups = (_pad_rows(_hm(w_lora_up[l]), LANES).astype(BF16),
                _pad_rows(_hm(a_lora_up[l]), LANES).astype(BF16),
                _hm(g_lora_up[l]).astype(BF16))

    wts = dict(
        ffn1_norm=row(ffn1_norm[l]), ffn1_gate=ffn1_gate[l].astype(BF16), ffn1_up=ffn1_up[l].astype(BF16),
        ffn1_down=ffn1_down[l].astype(BF16), mix_norm=row(mix_norm[l]), w_groups=w_groups,
        prep_vecs=prep_vecs, prep_ups=prep_ups,
        ln_g=row(_hm(ln_x_g[l])), ln_b=row(_hm(ln_x_b[l])), conv_w=_hm(conv_w[l]),
        w_out=_hm(w_out[l].T).T.astype(BF16),
        ffn2_norm=row(ffn2_norm[l]), ffn2_gate=ffn2_gate[l].astype(BF16), ffn2_up=ffn2_up[l].astype(BF16),
        ffn2_down=ffn2_down[l].astype(BF16), final_norm=row(final_norm),
    )

    b_p, t_p, _ = x_prompt.shape
    b_s, t_s, _ = x_sample.shape
    zero_state = jnp.zeros((b_p, N_HEADS, HEAD_DIM, HEAD_DIM), state_wkv.dtype)
    y_p, sh_p, cv_p, wk_p = _layer(x_prompt, None, None, zero_state, wts,
                                   tm_prep=128, tm_post=256, seg_prep=128, seg_post=256, tt=64)
    y_s, sh_s, cv_s, wk_s = _layer(x_sample, state_shift[l], state_conv[l], state_wkv[l], wts,
                                   tm_prep=128, tm_post=256, seg_prep=t_s, seg_post=t_s, tt=t_s)
    return (y_p, y_s, sh_p[None], cv_p[None], wk_p[None], sh_s[None], cv_s[None], wk_s[None])
```

```python
import functools
import math

import jax
import jax.numpy as jnp
from jax import lax
from jax.experimental import pallas as pl
from jax.experimental.pallas import tpu as pltpu

F32 = jnp.float32
BF16 = jnp.bfloat16

HEAD_DIM = 64
N_HEADS = 32
D_MODEL = N_HEADS * HEAD_DIM
CONV_W = 3
LORA_W = 96
LORA_A = 96
LORA_G = 256
RMS_EPS = 1e-6
GN_EPS = 64e-5
EXP_M05 = math.exp(-0.5)

LANES = 128
SUBLANES = 8
N_SLABS = D_MODEL // LANES
K_PER_SLAB = LANES // N_HEADS
LORA_PAD = 512
N_GROUPS = 8
N_PROJ = N_GROUPS * D_MODEL + LORA_PAD
VMEM_LIMIT = 56 * 1024 * 1024
FFN_TM = 512
PROJ_TM = 2048


def _cparams(sem):
    return pltpu.CompilerParams(dimension_semantics=sem, vmem_limit_bytes=VMEM_LIMIT)


def _sigmoid(x):
    return 1.0 / (1.0 + jnp.exp(-x))


def _ffn_body(x_ref, g1_ref, wg_ref, wu_ref, wd_ref, g2_ref, *rest, seg):
    if seg is None:
        ho_ref, xn_ref, acc_ref = rest
    else:
        xo_ref, ho_ref, hl_ref, xn_ref, acc_ref = rest
    j = pl.program_id(1)

    @pl.when(j == 0)
    def _():
        x = x_ref[...]
        ms = jnp.mean(x * x, axis=-1, keepdims=True)
        xn_ref[...] = (x * lax.rsqrt(ms + RMS_EPS) * g1_ref[...]).astype(BF16)
        acc_ref[...] = jnp.zeros_like(acc_ref)

    xn = xn_ref[...]
    gate = jnp.dot(xn, wg_ref[...], preferred_element_type=F32)
    up = jnp.dot(xn, wu_ref[...], preferred_element_type=F32)
    act = (gate * _sigmoid(gate) * up).astype(BF16)
    acc_ref[...] += jnp.dot(act, wd_ref[...], preferred_element_type=F32)

    @pl.when(j == pl.num_programs(1) - 1)
    def _():
        x1 = x_ref[...] + 0.5 * acc_ref[...]
        ms = jnp.mean(x1 * x1, axis=-1, keepdims=True)
        h = x1 * lax.rsqrt(ms + RMS_EPS) * g2_ref[...]
        if seg is None:
            ho_ref[...] = h
        else:
            xo_ref[...] = x1
            ho_ref[...] = h.astype(BF16)
            hl_ref[...] = jnp.zeros_like(hl_ref)
            for s in range(x1.shape[0] // seg):
                hl_ref[s:s + 1, :] = h[(s + 1) * seg - 1:(s + 1) * seg, :]


def _ffn(x, g1, wg, wu, wd, g2, *, seg, tm=FFN_TM, tf=512):
    m, d = x.shape
    f = wg.shape[1]
    grid = (m // tm, f // tf)
    row = pl.BlockSpec((tm, d), lambda i, j: (i, 0))
    vec = pl.BlockSpec((1, d), lambda i, j: (0, 0))
    if seg is None:
        out_specs = [row]
        out_shape = [jax.ShapeDtypeStruct((m, d), F32)]
    else:
        assert tm // seg <= SUBLANES
        out_specs = [row, row, pl.BlockSpec((SUBLANES, d), lambda i, j: (i, 0))]
        out_shape = [jax.ShapeDtypeStruct((m, d), F32), jax.ShapeDtypeStruct((m, d), BF16),
                     jax.ShapeDtypeStruct((m // tm * SUBLANES, d), F32)]
    return pl.pallas_call(
        functools.partial(_ffn_body, seg=seg),
        grid=grid,
        in_specs=[row, vec,
                  pl.BlockSpec((d, tf), lambda i, j: (0, j)),
                  pl.BlockSpec((d, tf), lambda i, j: (0, j)),
                  pl.BlockSpec((tf, d), lambda i, j: (j, 0)),
                  vec],
        out_specs=out_specs,
        out_shape=out_shape,
        scratch_shapes=[pltpu.VMEM((tm, d), BF16), pltpu.VMEM((tm, d), F32)],
        compiler_params=_cparams(("parallel", "arbitrary")),
        name="ffn",
    )(x, g1, wg, wu, wd, g2)


def _proj_body(h_ref, *refs, starts):
    w_refs, o_ref = refs[:-1], refs[-1]
    j = pl.program_id(1)
    for g, w_ref in enumerate(w_refs):
        @pl.when((j >= starts[g]) & (j < starts[g + 1]))
        def _(w_ref=w_ref):
            o_ref[...] = jnp.dot(h_ref[...], w_ref[...], preferred_element_type=F32)


def _proj(h, ws, *, tm, tn=512):
    m, d = h.shape
    starts = [0]
    for w in ws:
        starts.append(starts[-1] + w.shape[1] // tn)
    w_specs = [pl.BlockSpec((d, tn), lambda i, j, lo=lo, hi=hi: (0, jnp.clip(j - lo, 0, hi - lo - 1)))
               for lo, hi in zip(starts[:-1], starts[1:])]
    return pl.pallas_call(
        functools.partial(_proj_body, starts=tuple(starts)),
        grid=(m // tm, starts[-1]),
        in_specs=[pl.BlockSpec((tm, d), lambda i, j: (i, 0))] + w_specs,
        out_specs=pl.BlockSpec((tm, tn), lambda i, j: (i, j)),
        out_shape=jax.ShapeDtypeStruct((m, starts[-1] * tn), F32),
        compiler_params=_cparams(("parallel", "arbitrary")),
        name="proj",
    )(h, *ws)


def _slab(x, q):
    return x[:, q * LANES:(q + 1) * LANES]


def _head_sum(x):
    acc = _slab(x, 0)
    for q in range(1, N_SLABS):
        acc = acc + _slab(x, q)
    acc = acc + pltpu.roll(acc, 2 * N_HEADS, axis=1)
    return acc + pltpu.roll(acc, N_HEADS, axis=1)


def _tile_slabs(x):
    return jnp.concatenate([x] * N_SLABS, axis=1)


def _prep_body(pr_ref, pk_ref, pv_ref, pl_ref, br_ref, bk_ref, bv_ref, bl_ref,
               mur_ref, muk_ref, muv_ref, mul_ref, w0_ref, a0_ref, kkp_ref, kap_ref, rkp_ref,
               wup_ref, aup_ref,
               r_o, w_o, k_o, v_o, kk_o, kka_o, gl_o, bc_o, *, seg):
    tm = pr_ref.shape[0]
    nseg = tm // seg
    row = lax.broadcasted_iota(jnp.int32, (tm, 1), 0)

    def shifted(p_ref, b_ref, mu_ref):
        p = p_ref[...]
        prev = pltpu.roll(p, 1, axis=0)
        for s in range(nseg):
            prev = jnp.where(row == s * seg, b_ref[s:s + 1, :], prev)
        return p + mu_ref[...] * (prev - p)

    r = shifted(pr_ref, br_ref, mur_ref)
    k = shifted(pk_ref, bk_ref, muk_ref)
    v = shifted(pv_ref, bv_ref, muv_ref)
    lo = shifted(pl_ref, bl_ref, mul_ref)

    wl = jnp.tanh(lo[:, 0:LANES]).astype(BF16)
    al = lo[:, LANES:2 * LANES].astype(BF16)
    z = w0_ref[...] + jnp.dot(wl, wup_ref[...], preferred_element_type=F32)
    decay = jnp.exp(-EXP_M05 * _sigmoid(z))
    a = _sigmoid(a0_ref[...] + jnp.dot(al, aup_ref[...], preferred_element_type=F32))

    kk = k * kkp_ref[...]
    norm = jnp.maximum(jnp.sqrt(_head_sum(kk * kk)), 1e-12)
    kk = kk * _tile_slabs(1.0 / norm)
    k_mod = k * (1.0 + (a - 1.0) * kap_ref[...])

    pos = row % SUBLANES
    q = decay
    for sh in (1, 2, 4):
        q = q * jnp.where(pos >= sh, pltpu.roll(q, sh, axis=0), 1.0)
    inv_q = 1.0 / q

    r_o[...] = r * q
    w_o[...] = q
    k_o[...] = k_mod * inv_q
    v_o[...] = v
    kk_o[...] = kk
    kka_o[...] = kk * a * inv_q
    gl_o[...] = _sigmoid(lo[:, 2 * LANES:LORA_PAD]).astype(BF16)
    bc_o[...] = _head_sum(r * k_mod * rkp_ref[...])


def _prep(p, bnd, vecs, ups, *, tm, seg):
    m = p.shape[0]
    d = D_MODEL
    lora_blk = N_GROUPS * d // LORA_PAD
    grp = lambda g: pl.BlockSpec((tm, d), lambda i, g=g: (i, g))
    bgrp = lambda g: pl.BlockSpec((SUBLANES, d), lambda i, g=g: (i, g))
    vec = pl.BlockSpec((1, d), lambda i: (0, 0))
    row_out = pl.BlockSpec((tm, d), lambda i: (i, 0))
    mur, muk, muv, mul, w0, a0, kkp, kap, rkp = vecs
    wup, aup = ups
    return pl.pallas_call(
        functools.partial(_prep_body, seg=seg),
        grid=(m // tm,),
        in_specs=[grp(0), grp(1), grp(2),
                  pl.BlockSpec((tm, LORA_PAD), lambda i: (i, lora_blk)),
                  bgrp(0), bgrp(1), bgrp(2),
                  pl.BlockSpec((SUBLANES, LORA_PAD), lambda i: (i, 3 * d // LORA_PAD)),
                  vec, vec, vec, pl.BlockSpec((1, LORA_PAD), lambda i: (0, 0)),
                  vec, vec, vec, vec, vec,
                  pl.BlockSpec((LANES, d), lambda i: (0, 0)),
                  pl.BlockSpec((LANES, d), lambda i: (0, 0))],
        out_specs=[row_out] * 6 + [pl.BlockSpec((tm, LORA_G), lambda i: (i, 0)),
                                   pl.BlockSpec((tm, LANES), lambda i: (i, 0))],
        out_shape=[jax.ShapeDtypeStruct((m, d), F32)] * 6 + [jax.ShapeDtypeStruct((m, LORA_G), BF16),
                                                             jax.ShapeDtypeStruct((m, LANES), F32)],
        compiler_params=_cparams(("parallel",)),
        name="prep",
    )(p, p, p, p, bnd, bnd, bnd, bnd, mur, muk, muv, mul, w0, a0, kkp, kap, rkp, wup, aup)


N_VT = HEAD_DIM // SUBLANES
N_KQ = N_SLABS
PAIR = 2


def _group_sum(x):
    x = x + pltpu.roll(x, 2 * N_HEADS, axis=1)
    return x + pltpu.roll(x, N_HEADS, axis=1)


def _scan_body(r_ref, w_ref, k_ref, v_ref, kk_ref, kka_ref, kkh_ref, s0_ref, y_ref, s_ref, sa_ref, *, tt):
    sub = lax.broadcasted_iota(jnp.int32, (SUBLANES, LANES), 0)
    grp = lax.broadcasted_iota(jnp.int32, (SUBLANES, LANES), 1) // N_HEADS
    rot = (grp - sub % K_PER_SLAB) % K_PER_SLAB
    upper = sub >= K_PER_SLAB
    diag = grp == sub % K_PER_SLAB

    def tile8(ref, b, t8):
        start = t8 * SUBLANES
        if not isinstance(t8, int):
            start = pl.multiple_of(start, SUBLANES)
        return ref.at[b, pl.ds(start, SUBLANES), :]

    def row(view, s, q):
        return jnp.broadcast_to(view[pl.ds(s, 1), pl.ds(q * LANES, LANES)], (SUBLANES, LANES))

    @pl.when(pl.program_id(1) == 0)
    def _():
        s_ref[...] = s0_ref[...]
        for b in range(PAIR):
            kk0 = tile8(kk_ref, b, 0)
            for j in range(N_VT):
                acc = s0_ref[b, j, 0] * row(kk0, 0, 0)
                for kq in range(1, N_KQ):
                    acc = acc + s0_ref[b, j, kq] * row(kk0, 0, kq)
                sa_ref[b, j] = _group_sum(acc)

    def step(b, views, s, js):
        r_v, w_v, k_v, v_v, kk_v, kka_v, kkn_v, y_v = views
        vcols = {}
        for j in js:
            base = jnp.where(upper, row(v_v, s, 2 * j + 1), row(v_v, s, 2 * j))
            col = base
            for d in range(1, K_PER_SLAB):
                col = jnp.where(rot == d, pltpu.roll(base, d * N_HEADS, axis=1), col)
            vcols[j] = col
        sas = {j: sa_ref[b, j] for j in js}
        acc_y = {}
        acc_s = {}
        for kq in range(N_KQ):
            q_row = row(w_v, s, kq)
            kka_row = row(kka_v, s, kq)
            k_row = row(k_v, s, kq)
            r_row = row(r_v, s, kq)
            kkn_row = row(kk_v, s + 1, kq) if s + 1 < SUBLANES else row(kkn_v, 0, kq)
            qkn_row = q_row * kkn_row
            for j in js:
                s_new = s_ref[b, j, kq] - sas[j] * kka_row + vcols[j] * k_row
                s_ref[b, j, kq] = s_new * q_row if s == SUBLANES - 1 else s_new
                ty = s_new * r_row
                ts = s_new * qkn_row
                acc_y[j] = ty if kq == 0 else acc_y[j] + ty
                acc_s[j] = ts if kq == 0 else acc_s[j] + ts
        for j in js:
            sa_ref[b, j] = _group_sum(acc_s[j])
            yv = jnp.where(diag, _group_sum(acc_y[j]), 0.0)
            yv = yv + pltpu.roll(yv, 1, axis=0)
            yv = yv + pltpu.roll(yv, 2, axis=0)
            y_v[pl.ds(s, 1), pl.ds(2 * j * LANES, LANES)] = yv[K_PER_SLAB - 1:K_PER_SLAB, :]
            y_v[pl.ds(s, 1), pl.ds((2 * j + 1) * LANES, LANES)] = yv[SUBLANES - 1:SUBLANES, :]

    n8 = tt // SUBLANES

    halves = (tuple(range(N_VT // 2)), tuple(range(N_VT // 2, N_VT)))

    def steps8(t8, last):
        views = []
        for b in range(PAIR):
            vb = tuple(tile8(ref, b, t8) for ref in (r_ref, w_ref, k_ref, v_ref, kk_ref, kka_ref))
            nxt = kkh_ref.at[b] if last else tile8(kk_ref, b, t8 + 1)
            views.append(vb + (nxt, tile8(y_ref, b, t8)))
        for s in range(SUBLANES):
            for js in halves:
                for b in range(PAIR):
                    step(b, views[b], s, js)

    def loop_body(t8, carry):
        steps8(t8, False)
        return carry

    lax.fori_loop(0, n8 - 1, loop_body, 0)
    steps8(n8 - 1, True)


def _scan(r, w, k, v, kk, kka, s0, *, tt):
    bsz, t_len, d = r.shape
    seq = pl.BlockSpec((PAIR, tt, d), lambda i, j: (i, j, 0))
    st = pl.BlockSpec((PAIR, N_VT, N_KQ, SUBLANES, LANES), lambda i, j: (i, 0, 0, 0, 0))
    n8 = tt // SUBLANES
    last8 = t_len // SUBLANES - 1
    head = pl.BlockSpec((PAIR, SUBLANES, d), lambda i, j: (i, jnp.minimum((j + 1) * n8, last8), 0))
    return pl.pallas_call(
        functools.partial(_scan_body, tt=tt),
        grid=(bsz // PAIR, t_len // tt),
        in_specs=[seq] * 6 + [head, st],
        out_specs=[seq, st],
        out_shape=[jax.ShapeDtypeStruct((bsz, t_len, d), F32),
                   jax.ShapeDtypeStruct(s0.shape, F32)],
        scratch_shapes=[pltpu.VMEM((PAIR, N_VT, SUBLANES, LANES), F32)],
        compiler_params=_cparams(("parallel", "arbitrary")),
        name="wkv_scan",
    )(r, w, k, v, kk, kka, kk, s0)


def _state_to_tiles(s):
    b = s.shape[0]
    s = s.reshape(b, N_HEADS, N_VT, SUBLANES, N_KQ, K_PER_SLAB)
    return s.transpose(0, 2, 4, 3, 5, 1).reshape(b, N_VT, N_KQ, SUBLANES, LANES)


def _tiles_to_state(s):
    b = s.shape[0]
    s = s.reshape(b, N_VT, N_KQ, SUBLANES, K_PER_SLAB, N_HEADS)
    return s.transpose(0, 5, 1, 3, 2, 4).reshape(b, N_HEADS, HEAD_DIM, HEAD_DIM)


def _post_body(y_ref, v_ref, bc_ref, gl_ref, gtb_ref, gtc_ref, xc_ref, ga_ref, gb_ref, x1_ref,
               bgc_ref, bxc_ref, lng_ref, lnb_ref, cw_ref, gup_ref, wout_ref, x2_o, ut_o, *, seg):
    tm = y_ref.shape[0]
    nseg = tm // seg
    row = lax.broadcasted_iota(jnp.int32, (tm, 1), 0)
    inv_n = 1.0 / HEAD_DIM

    o = y_ref[...]
    mu = _head_sum(o) * inv_n
    c = o - _tile_slabs(mu)
    var = _head_sum(c * c) * inv_n
    o = c * _tile_slabs(lax.rsqrt(var + GN_EPS)) * lng_ref[...] + lnb_ref[...]
    bonus = _tile_slabs(bc_ref[...]) * v_ref[...]
    g = jnp.dot(gl_ref[...], gup_ref[...], preferred_element_type=F32)
    y_a = (o + bonus) * g

    u = gtc_ref[...] * xc_ref[...]
    ub = bgc_ref[...] * bxc_ref[...]
    u1 = pltpu.roll(u, 1, axis=0)
    u2 = pltpu.roll(u, 2, axis=0)
    for s in range(nseg):
        u1 = jnp.where(row == s * seg, ub[2 * s + 1:2 * s + 2, :], u1)
        u2 = jnp.where(row == s * seg, ub[2 * s:2 * s + 1, :], u2)
        u2 = jnp.where(row == s * seg + 1, ub[2 * s + 1:2 * s + 2, :], u2)
    conv = cw_ref[0:1, :] * u2 + cw_ref[1:2, :] * u1 + cw_ref[2:3, :] * u
    y_b = gtb_ref[...] * conv

    merged = _sigmoid(ga_ref[...]) * y_a + _sigmoid(gb_ref[...]) * y_b
    x2_o[...] = x1_ref[...] + jnp.dot(merged.astype(BF16), wout_ref[...], preferred_element_type=F32)

    ut_o[...] = jnp.zeros_like(ut_o)
    for s in range(nseg):
        ut_o[2 * s:2 * s + 2, :] = u[(s + 1) * seg - 2:(s + 1) * seg, :]


def _post(y, v, bc, gl, p, x1, bgc, bxc, lng, lnb, cw, gup, wout, *, tm, seg):
    m, d = y.shape
    rowb = pl.BlockSpec((tm, d), lambda i: (i, 0))
    grp = lambda gi: pl.BlockSpec((tm, d), lambda i, gi=gi: (i, gi))
    vec = pl.BlockSpec((1, d), lambda i: (0, 0))
    bnd = pl.BlockSpec((SUBLANES, d), lambda i: (i, 0))
    return pl.pallas_call(
        functools.partial(_post_body, seg=seg),
        grid=(m // tm,),
        in_specs=[rowb, rowb, pl.BlockSpec((tm, LANES), lambda i: (i, 0)),
                  pl.BlockSpec((tm, LORA_G), lambda i: (i, 0)),
                  grp(3), grp(4), grp(5), grp(6), grp(7), rowb,
                  bnd, bnd, vec, vec,
                  pl.BlockSpec((CONV_W, d), lambda i: (0, 0)),
                  pl.BlockSpec((LORA_G, d), lambda i: (0, 0)),
                  pl.BlockSpec((d, d), lambda i: (0, 0), pipeline_mode=pl.Buffered(1))],
        out_specs=[rowb, bnd],
        out_shape=[jax.ShapeDtypeStruct((m, d), F32),
                   jax.ShapeDtypeStruct((m // tm * SUBLANES, d), F32)],
        compiler_params=_cparams(("parallel",)),
        name="post",
    )(y, v, bc, gl, p, p, p, p, p, x1, bgc, bxc, lng, lnb, cw, gup, wout)


def _hm(x):
    lead = x.shape[:-1]
    return x.reshape(*lead, N_HEADS, HEAD_DIM).swapaxes(-1, -2).reshape(*lead, D_MODEL)


def _hm_groups(x):
    lead = x.shape[:-1]
    n = x.shape[-1] // D_MODEL
    return x.reshape(*lead, n, N_HEADS, HEAD_DIM).swapaxes(-1, -2).reshape(*lead, n * D_MODEL)


def _hm_inv(x):
    lead = x.shape[:-1]
    return x.reshape(*lead, HEAD_DIM, N_HEADS).swapaxes(-1, -2).reshape(*lead, D_MODEL)


def _pad_lora(x):
    z = jnp.zeros(x.shape[:-1] + (LANES - LORA_W,), x.dtype)
    return jnp.concatenate([x[..., :LORA_W], z, x[..., LORA_W:LORA_W + LORA_A], z,
                            x[..., LORA_W + LORA_A:]], axis=-1)


def _pad_rows(x, rows):
    return jnp.concatenate([x, jnp.zeros((rows - x.shape[0],) + x.shape[1:], x.dtype)], axis=0)


def _tile_bounds(rows, n_tiles, n_used):
    w = rows.shape[-1]
    rows = rows.reshape(n_tiles, n_used, w)
    pad = jnp.zeros((n_tiles, SUBLANES - n_used, w), rows.dtype)
    return jnp.concatenate([rows, pad], axis=1).reshape(n_tiles * SUBLANES, w)


def _layer(x, shift_prev, conv_prev, wkv_prev, wts, *, tm_prep, tm_post, seg_prep, seg_post, tt):
    bsz, t_len, d = x.shape
    m = bsz * t_len
    xf = x.reshape(m, d)

    x1, h, h_last = _ffn(xf, wts["ffn1_norm"], wts["ffn1_gate"], wts["ffn1_up"], wts["ffn1_down"],
                         wts["mix_norm"], seg=min(t_len, FFN_TM))
    p = _proj(h, wts["w_groups"], tm=min(m, PROJ_TM))

    n_rkv = 3 * d
    n_tiles = m // tm_prep
    if shift_prev is None:
        starts = jnp.arange(n_tiles) * tm_prep
        prev = jnp.take(p, jnp.maximum(starts - 1, 0), axis=0)
        prev = jnp.where((starts % t_len == 0)[:, None], 0.0, prev)
        bnd = jnp.concatenate([prev[:, :n_rkv], prev[:, N_GROUPS * d:]], axis=1)
        n_used = 1
    else:
        rows = shift_prev.shape[0]
        ws = wts["w_groups"]
        bnd = _proj(_pad_rows(shift_prev, 32).astype(BF16), [ws[0], ws[2]], tm=32)[:rows]
        n_used = tm_prep // seg_prep
    bnd = _tile_bounds(bnd, n_tiles, n_used)

    r, w, k, v, kk, kka, gl, bc = _prep(p, bnd, wts["prep_vecs"], wts["prep_ups"],
                                       tm=tm_prep, seg=seg_prep)

    s0 = _state_to_tiles(wkv_prev)
    sh = (bsz, t_len, d)
    y, s_new = _scan(r.reshape(sh), w.reshape(sh), k.reshape(sh), v.reshape(sh),
                     kk.reshape(sh), kka.reshape(sh), s0, tt=tt)
    wkv_new = _tiles_to_state(s_new)

    n_tiles = m // tm_post
    if conv_prev is None:
        starts = jnp.arange(n_tiles) * tm_post
        idx = jnp.maximum(starts[:, None] + jnp.array([-2, -1])[None, :], 0).reshape(-1)
        rows = jnp.take(p, idx, axis=0)
        live = jnp.repeat(starts % t_len != 0, 2)[:, None]
        bgc = jnp.where(live, rows[:, 4 * d:5 * d], 0.0)
        bxc = rows[:, 5 * d:6 * d]
        n_used = 2
    else:
        bgc = _hm(conv_prev).reshape(bsz * (CONV_W - 1), d)
        bxc = jnp.ones_like(bgc)
        n_used = 2 * (tm_post // seg_post)
    bgc = _tile_bounds(bgc, n_tiles, n_used)
    bxc = _tile_bounds(bxc, n_tiles, n_used)

    x2, utail = _post(y.reshape(m, d), v, bc, gl, p, x1, bgc, bxc,
                      wts["ln_g"], wts["ln_b"], wts["conv_w"], wts["g_up"], wts["w_out"],
                      tm=tm_post, seg=seg_post)

    (yout,) = _ffn(x2, wts["ffn2_norm"], wts["ffn2_gate"], wts["ffn2_up"], wts["ffn2_down"],
                   wts["final_norm"], seg=None)

    segs = m // min(t_len, FFN_TM)
    h_last = h_last.reshape(m // FFN_TM, SUBLANES, d)[:, :segs // (m // FFN_TM)].reshape(segs, d)
    shift_new = h_last.reshape(bsz, segs // bsz, d)[:, -1]
    nseg_post = tm_post // seg_post
    ut = utail.reshape(n_tiles, SUBLANES, d)[:, :2 * nseg_post]
    ut = ut.reshape(n_tiles * nseg_post, 2, d)
    segs_per_seq = t_len // seg_post
    conv_new = _hm_inv(ut[segs_per_seq - 1::segs_per_seq])
    return yout.reshape(bsz, t_len, d), shift_new, conv_new, wkv_new


def kernel(x_prompt, x_sample, state_shift, state_conv, state_wkv, ffn1_norm, ffn1_gate, ffn1_up, ffn1_down, mix_norm, w_in, mu_shift, w0, w_lora_up, a0, a_lora_up, g_lora_up, k_k, k_a, r_k, ln_x_g, ln_x_b, conv_w, w_out, ffn2_norm, ffn2_gate, ffn2_up, ffn2_down, final_norm):
    depth = w_in.shape[0]
    assert depth == 1, "single-layer kernel"
    d = D_MODEL
    n_a = 3 * d + LORA_W + LORA_A + LORA_G
    l = 0

    wi = w_in[l]
    wb = wi.astype(BF16)
    w_groups = [_hm_groups(wb[:, :3 * d]), _hm_groups(wb[:, n_a:]), _pad_lora(wb[:, 3 * d:n_a])]

    mu = mu_shift[l]
    row = lambda x: x.reshape(1, -1)
    prep_vecs = (row(_hm(mu[0:d])), row(_hm(mu[d:2 * d])), row(_hm(mu[2 * d:3 * d])),
                 row(_pad_lora(mu[3 * d:])),
                 row(_hm(w0[l])), row(_hm(a0[l])), row(_hm(k_k[l])), row(_hm(k_a[l])),
                 row(_hm(r_k[l].reshape(d))))
    prep_ups = (_pad_rows(_hm(w_lora_up[l]), LANES).astype(BF16),
                _pad_rows(_hm(a_lora_up[l]), LANES).astype(BF16))

    wts = dict(
        ffn1_norm=row(ffn1_norm[l]), ffn1_gate=ffn1_gate[l].astype(BF16), ffn1_up=ffn1_up[l].astype(BF16),
        ffn1_down=ffn1_down[l].astype(BF16), mix_norm=row(mix_norm[l]), w_groups=w_groups,
        prep_vecs=prep_vecs, prep_ups=prep_ups, g_up=_hm(g_lora_up[l]).astype(BF16),
        ln_g=row(_hm(ln_x_g[l])), ln_b=row(_hm(ln_x_b[l])), conv_w=_hm(conv_w[l]),
        w_out=_hm(w_out[l].T).T.astype(BF16),
        ffn2_norm=row(ffn2_norm[l]), ffn2_gate=ffn2_gate[l].astype(BF16), ffn2_up=ffn2_up[l].astype(BF16),
        ffn2_down=ffn2_down[l].astype(BF16), final_norm=row(final_norm),
    )

    b_p, t_p, _ = x_prompt.shape
    b_s, t_s, _ = x_sample.shape
    zero_state = jnp.zeros((b_p, N_HEADS, HEAD_DIM, HEAD_DIM), state_wkv.dtype)
    y_p, sh_p, cv_p, wk_p = _layer(x_prompt, None, None, zero_state, wts,
                                   tm_prep=128, tm_post=256, seg_prep=128, seg_post=256, tt=64)
    y_s, sh_s, cv_s, wk_s = _layer(x_sample, state_shift[l], state_conv[l], state_wkv[l], wts,
                                   tm_prep=128, tm_post=256, seg_prep=t_s, seg_post=t_s, tt=t_s)
    return (y_p, y_s, sh_p[None], cv_p[None], wk_p[None], sh_s[None], cv_s[None], wk_s[None])
```

```python
import functools
import math

import jax
import jax.numpy as jnp
from jax import lax
from jax.experimental import pallas as pl
from jax.experimental.pallas import tpu as pltpu

F32 = jnp.float32
BF16 = jnp.bfloat16

HEAD_DIM = 64
N_HEADS = 32
D_MODEL = N_HEADS * HEAD_DIM
CONV_W = 3
LORA_W = 96
LORA_A = 96
LORA_G = 256
RMS_EPS = 1e-6
GN_EPS = 64e-5
EXP_M05 = math.exp(-0.5)

LANES = 128
SUBLANES = 8
N_SLABS = D_MODEL // LANES
K_PER_SLAB = LANES // N_HEADS
LORA_PAD = 512
N_GROUPS = 8
N_PROJ = N_GROUPS * D_MODEL + LORA_PAD
VMEM_LIMIT = 56 * 1024 * 1024
FFN_TM = 512
PROJ_TM = 2048


def _cparams(sem):
    return pltpu.CompilerParams(dimension_semantics=sem, vmem_limit_bytes=VMEM_LIMIT)


def _sigmoid(x):
    return 1.0 / (1.0 + jnp.exp(-x))


def _ffn_body(x_ref, g1_ref, wg_ref, wu_ref, wd_ref, g2_ref, *rest, seg):
    if seg is None:
        ho_ref, xn_ref, acc_ref = rest
    else:
        xo_ref, ho_ref, hl_ref, xn_ref, acc_ref = rest
    j = pl.program_id(1)

    @pl.when(j == 0)
    def _():
        x = x_ref[...]
        ms = jnp.mean(x * x, axis=-1, keepdims=True)
        xn_ref[...] = (x * lax.rsqrt(ms + RMS_EPS) * g1_ref[...]).astype(BF16)
        acc_ref[...] = jnp.zeros_like(acc_ref)

    xn = xn_ref[...]
    gate = jnp.dot(xn, wg_ref[...], preferred_element_type=F32)
    up = jnp.dot(xn, wu_ref[...], preferred_element_type=F32)
    act = (gate * _sigmoid(gate) * up).astype(BF16)
    acc_ref[...] += jnp.dot(act, wd_ref[...], preferred_element_type=F32)

    @pl.when(j == pl.num_programs(1) - 1)
    def _():
        x1 = x_ref[...] + 0.5 * acc_ref[...]
        ms = jnp.mean(x1 * x1, axis=-1, keepdims=True)
        h = x1 * lax.rsqrt(ms + RMS_EPS) * g2_ref[...]
        if seg is None:
            ho_ref[...] = h
        else:
            xo_ref[...] = x1
            ho_ref[...] = h.astype(BF16)
            hl_ref[...] = jnp.zeros_like(hl_ref)
            for s in range(x1.shape[0] // seg):
                hl_ref[s:s + 1, :] = h[(s + 1) * seg - 1:(s + 1) * seg, :]


def _ffn(x, g1, wg, wu, wd, g2, *, seg, tm=FFN_TM, tf=512):
    m, d = x.shape
    f = wg.shape[1]
    grid = (m // tm, f // tf)
    row = pl.BlockSpec((tm, d), lambda i, j: (i, 0))
    vec = pl.BlockSpec((1, d), lambda i, j: (0, 0))
    if seg is None:
        out_specs = [row]
        out_shape = [jax.ShapeDtypeStruct((m, d), F32)]
    else:
        assert tm // seg <= SUBLANES
        out_specs = [row, row, pl.BlockSpec((SUBLANES, d), lambda i, j: (i, 0))]
        out_shape = [jax.ShapeDtypeStruct((m, d), F32), jax.ShapeDtypeStruct((m, d), BF16),
                     jax.ShapeDtypeStruct((m // tm * SUBLANES, d), F32)]
    return pl.pallas_call(
        functools.partial(_ffn_body, seg=seg),
        grid=grid,
        in_specs=[row, vec,
                  pl.BlockSpec((d, tf), lambda i, j: (0, j)),
                  pl.BlockSpec((d, tf), lambda i, j: (0, j)),
                  pl.BlockSpec((tf, d), lambda i, j: (j, 0)),
                  vec],
        out_specs=out_specs,
        out_shape=out_shape,
        scratch_shapes=[pltpu.VMEM((tm, d), BF16), pltpu.VMEM((tm, d), F32)],
        compiler_params=_cparams(("parallel", "arbitrary")),
        name="ffn",
    )(x, g1, wg, wu, wd, g2)


def _proj_body(h_ref, *refs, starts):
    w_refs, o_ref = refs[:-1], refs[-1]
    j = pl.program_id(1)
    for g, w_ref in enumerate(w_refs):
        @pl.when((j >= starts[g]) & (j < starts[g + 1]))
        def _(w_ref=w_ref):
            o_ref[...] = jnp.dot(h_ref[...], w_ref[...], preferred_element_type=F32)


def _proj(h, ws, *, tm, tn=512):
    m, d = h.shape
    starts = [0]
    for w in ws:
        starts.append(starts[-1] + w.shape[1] // tn)
    w_specs = [pl.BlockSpec((d, tn), lambda i, j, lo=lo, hi=hi: (0, jnp.clip(j - lo, 0, hi - lo - 1)))
               for lo, hi in zip(starts[:-1], starts[1:])]
    return pl.pallas_call(
        functools.partial(_proj_body, starts=tuple(starts)),
        grid=(m // tm, starts[-1]),
        in_specs=[pl.BlockSpec((tm, d), lambda i, j: (i, 0))] + w_specs,
        out_specs=pl.BlockSpec((tm, tn), lambda i, j: (i, j)),
        out_shape=jax.ShapeDtypeStruct((m, starts[-1] * tn), F32),
        compiler_params=_cparams(("parallel", "arbitrary")),
        name="proj",
    )(h, *ws)


def _slab(x, q):
    return x[:, q * LANES:(q + 1) * LANES]


def _head_sum(x):
    acc = _slab(x, 0)
    for q in range(1, N_SLABS):
        acc = acc + _slab(x, q)
    acc = acc + pltpu.roll(acc, 2 * N_HEADS, axis=1)
    return acc + pltpu.roll(acc, N_HEADS, axis=1)


def _tile_slabs(x):
    return jnp.concatenate([x] * N_SLABS, axis=1)


def _prep_body(pr_ref, pk_ref, pv_ref, pl_ref, br_ref, bk_ref, bv_ref, bl_ref,
               mur_ref, muk_ref, muv_ref, mul_ref, w0_ref, a0_ref, kkp_ref, kap_ref, rkp_ref,
               wup_ref, aup_ref,
               r_o, w_o, k_o, v_o, kk_o, kka_o, gl_o, bc_o, *, seg):
    tm = pr_ref.shape[0]
    nseg = tm // seg
    row = lax.broadcasted_iota(jnp.int32, (tm, 1), 0)

    def shifted(p_ref, b_ref, mu_ref):
        p = p_ref[...]
        prev = pltpu.roll(p, 1, axis=0)
        for s in range(nseg):
            prev = jnp.where(row == s * seg, b_ref[s:s + 1, :], prev)
        return p + mu_ref[...] * (prev - p)

    r = shifted(pr_ref, br_ref, mur_ref)
    k = shifted(pk_ref, bk_ref, muk_ref)
    v = shifted(pv_ref, bv_ref, muv_ref)
    lo = shifted(pl_ref, bl_ref, mul_ref)

    wl = jnp.tanh(lo[:, 0:LANES]).astype(BF16)
    al = lo[:, LANES:2 * LANES].astype(BF16)
    z = w0_ref[...] + jnp.dot(wl, wup_ref[...], preferred_element_type=F32)
    decay = jnp.exp(-EXP_M05 * _sigmoid(z))
    a = _sigmoid(a0_ref[...] + jnp.dot(al, aup_ref[...], preferred_element_type=F32))

    kk = k * kkp_ref[...]
    norm = jnp.maximum(jnp.sqrt(_head_sum(kk * kk)), 1e-12)
    kk = kk * _tile_slabs(1.0 / norm)
    k_mod = k * (1.0 + (a - 1.0) * kap_ref[...])

    pos = row % SUBLANES
    q = decay
    for sh in (1, 2, 4):
        q = q * jnp.where(pos >= sh, pltpu.roll(q, sh, axis=0), 1.0)
    inv_q = 1.0 / q

    r_o[...] = r * q
    w_o[...] = q
    k_o[...] = k_mod * inv_q
    v_o[...] = v
    kk_o[...] = kk
    kka_o[...] = kk * a * inv_q
    gl_o[...] = _sigmoid(lo[:, 2 * LANES:LORA_PAD]).astype(BF16)
    bc_o[...] = _head_sum(r * k_mod * rkp_ref[...])


def _prep(p, bnd, vecs, ups, *, tm, seg):
    m = p.shape[0]
    d = D_MODEL
    lora_blk = N_GROUPS * d // LORA_PAD
    grp = lambda g: pl.BlockSpec((tm, d), lambda i, g=g: (i, g))
    bgrp = lambda g: pl.BlockSpec((SUBLANES, d), lambda i, g=g: (i, g))
    vec = pl.BlockSpec((1, d), lambda i: (0, 0))
    row_out = pl.BlockSpec((tm, d), lambda i: (i, 0))
    mur, muk, muv, mul, w0, a0, kkp, kap, rkp = vecs
    wup, aup = ups
    return pl.pallas_call(
        functools.partial(_prep_body, seg=seg),
        grid=(m // tm,),
        in_specs=[grp(0), grp(1), grp(2),
                  pl.BlockSpec((tm, LORA_PAD), lambda i: (i, lora_blk)),
                  bgrp(0), bgrp(1), bgrp(2),
                  pl.BlockSpec((SUBLANES, LORA_PAD), lambda i: (i, 3 * d // LORA_PAD)),
                  vec, vec, vec, pl.BlockSpec((1, LORA_PAD), lambda i: (0, 0)),
                  vec, vec, vec, vec, vec,
                  pl.BlockSpec((LANES, d), lambda i: (0, 0)),
                  pl.BlockSpec((LANES, d), lambda i: (0, 0))],
        out_specs=[row_out] * 6 + [pl.BlockSpec((tm, LORA_G), lambda i: (i, 0)),
                                   pl.BlockSpec((tm, LANES), lambda i: (i, 0))],
        out_shape=[jax.ShapeDtypeStruct((m, d), F32)] * 6 + [jax.ShapeDtypeStruct((m, LORA_G), BF16),
                                                             jax.ShapeDtypeStruct((m, LANES), F32)],
        compiler_params=_cparams(("parallel",)),
        name="prep",
    )(p, p, p, p, bnd, bnd, bnd, bnd, mur, muk, muv, mul, w0, a0, kkp, kap, rkp, wup, aup)


N_VT = HEAD_DIM // SUBLANES
N_KQ = N_SLABS
PAIR = 2


def _group_sum(x):
    x = x + pltpu.roll(x, 2 * N_HEADS, axis=1)
    return x + pltpu.roll(x, N_HEADS, axis=1)


def _scan_body(r_ref, w_ref, k_ref, v_ref, kk_ref, kka_ref, kkh_ref, s0_ref, y_ref, s_ref, sa_ref, *, tt):
    sub = lax.broadcasted_iota(jnp.int32, (SUBLANES, LANES), 0)
    grp = lax.broadcasted_iota(jnp.int32, (SUBLANES, LANES), 1) // N_HEADS
    rot = (grp - sub % K_PER_SLAB) % K_PER_SLAB
    upper = sub >= K_PER_SLAB
    diag = grp == sub % K_PER_SLAB

    def tile8(ref, b, t8):
        start = t8 * SUBLANES
        if not isinstance(t8, int):
            start = pl.multiple_of(start, SUBLANES)
        return ref.at[b, pl.ds(start, SUBLANES), :]

    def row(view, s, q):
        return jnp.broadcast_to(view[pl.ds(s, 1), pl.ds(q * LANES, LANES)], (SUBLANES, LANES))

    @pl.when(pl.program_id(1) == 0)
    def _():
        s_ref[...] = s0_ref[...]
        for b in range(PAIR):
            kk0 = tile8(kk_ref, b, 0)
            for j in range(N_VT):
                acc = s0_ref[b, j, 0] * row(kk0, 0, 0)
                for kq in range(1, N_KQ):
                    acc = acc + s0_ref[b, j, kq] * row(kk0, 0, kq)
                sa_ref[b, j] = _group_sum(acc)

    def step(b, views, s, js):
        r_v, w_v, k_v, v_v, kk_v, kka_v, kkn_v, y_v = views
        vcols = {}
        for j in js:
            base = jnp.where(upper, row(v_v, s, 2 * j + 1), row(v_v, s, 2 * j))
            col = base
            for d in range(1, K_PER_SLAB):
                col = jnp.where(rot == d, pltpu.roll(base, d * N_HEADS, axis=1), col)
            vcols[j] = col
        sas = {j: sa_ref[b, j] for j in js}
        acc_y = {}
        acc_s = {}
        for kq in range(N_KQ):
            q_row = row(w_v, s, kq)
            kka_row = row(kka_v, s, kq)
            k_row = row(k_v, s, kq)
            r_row = row(r_v, s, kq)
            kkn_row = row(kk_v, s + 1, kq) if s + 1 < SUBLANES else row(kkn_v, 0, kq)
            qkn_row = q_row * kkn_row
            for j in js:
                s_new = s_ref[b, j, kq] - sas[j] * kka_row + vcols[j] * k_row
                s_ref[b, j, kq] = s_new * q_row if s == SUBLANES - 1 else s_new
                ty = s_new * r_row
                ts = s_new * qkn_row
                acc_y[j] = ty if kq == 0 else acc_y[j] + ty
                acc_s[j] = ts if kq == 0 else acc_s[j] + ts
        for j in js:
            sa_ref[b, j] = _group_sum(acc_s[j])
            yv = jnp.where(diag, _group_sum(acc_y[j]), 0.0)
            yv = yv + pltpu.roll(yv, 1, axis=0)
            yv = yv + pltpu.roll(yv, 2, axis=0)
            y_v[pl.ds(s, 1), pl.ds(2 * j * LANES, LANES)] = yv[K_PER_SLAB - 1:K_PER_SLAB, :]
            y_v[pl.ds(s, 1), pl.ds((2 * j + 1) * LANES, LANES)] = yv[SUBLANES - 1:SUBLANES, :]

    n8 = tt // SUBLANES

    halves = (tuple(range(N_VT // 2)), tuple(range(N_VT // 2, N_VT)))

    def steps8(t8, last):
        views = []
        for b in range(PAIR):
            vb = tuple(tile8(ref, b, t8) for ref in (r_ref, w_ref, k_ref, v_ref, kk_ref, kka_ref))
            nxt = kkh_ref.at[b] if last else tile8(kk_ref, b, t8 + 1)
            views.append(vb + (nxt, tile8(y_ref, b, t8)))
        for s in range(SUBLANES):
            for b in range(PAIR):
                for js in halves:
                    step(b, views[b], s, js)

    def loop_body(t8, carry):
        steps8(t8, False)
        return carry

    lax.fori_loop(0, n8 - 1, loop_body, 0)
    steps8(n8 - 1, True)


def _scan(r, w, k, v, kk, kka, s0, *, tt):
    bsz, t_len, d = r.shape
    seq = pl.BlockSpec((PAIR, tt, d), lambda i, j: (i, j, 0))
    st = pl.BlockSpec((PAIR, N_VT, N_KQ, SUBLANES, LANES), lambda i, j: (i, 0, 0, 0, 0))
    n8 = tt // SUBLANES
    last8 = t_len // SUBLANES - 1
    head = pl.BlockSpec((PAIR, SUBLANES, d), lambda i, j: (i, jnp.minimum((j + 1) * n8, last8), 0))
    return pl.pallas_call(
        functools.partial(_scan_body, tt=tt),
        grid=(bsz // PAIR, t_len // tt),
        in_specs=[seq] * 6 + [head, st],
        out_specs=[seq, st],
        out_shape=[jax.ShapeDtypeStruct((bsz, t_len, d), F32),
                   jax.ShapeDtypeStruct(s0.shape, F32)],
        scratch_shapes=[pltpu.VMEM((PAIR, N_VT, SUBLANES, LANES), F32)],
        compiler_params=_cparams(("parallel", "arbitrary")),
        name="wkv_scan",
    )(r, w, k, v, kk, kka, kk, s0)


def _state_to_tiles(s):
    b = s.shape[0]
    s = s.reshape(b, N_HEADS, N_VT, SUBLANES, N_KQ, K_PER_SLAB)
    return s.transpose(0, 2, 4, 3, 5, 1).reshape(b, N_VT, N_KQ, SUBLANES, LANES)


def _tiles_to_state(s):
    b = s.shape[0]
    s = s.reshape(b, N_VT, N_KQ, SUBLANES, K_PER_SLAB, N_HEADS)
    return s.transpose(0, 5, 1, 3, 2, 4).reshape(b, N_HEADS, HEAD_DIM, HEAD_DIM)


def _post_body(y_ref, v_ref, bc_ref, gl_ref, gtb_ref, gtc_ref, xc_ref, ga_ref, gb_ref, x1_ref,
               bgc_ref, bxc_ref, lng_ref, lnb_ref, cw_ref, gup_ref, wout_ref, x2_o, ut_o, *, seg):
    tm = y_ref.shape[0]
    nseg = tm // seg
    row = lax.broadcasted_iota(jnp.int32, (tm, 1), 0)
    inv_n = 1.0 / HEAD_DIM

    o = y_ref[...]
    mu = _head_sum(o) * inv_n
    c = o - _tile_slabs(mu)
    var = _head_sum(c * c) * inv_n
    o = c * _tile_slabs(lax.rsqrt(var + GN_EPS)) * lng_ref[...] + lnb_ref[...]
    bonus = _tile_slabs(bc_ref[...]) * v_ref[...]
    g = jnp.dot(gl_ref[...], gup_ref[...], preferred_element_type=F32)
    y_a = (o + bonus) * g

    u = gtc_ref[...] * xc_ref[...]
    ub = bgc_ref[...] * bxc_ref[...]
    u1 = pltpu.roll(u, 1, axis=0)
    u2 = pltpu.roll(u, 2, axis=0)
    for s in range(nseg):
        u1 = jnp.where(row == s * seg, ub[2 * s + 1:2 * s + 2, :], u1)
        u2 = jnp.where(row == s * seg, ub[2 * s:2 * s + 1, :], u2)
        u2 = jnp.where(row == s * seg + 1, ub[2 * s + 1:2 * s + 2, :], u2)
    conv = cw_ref[0:1, :] * u2 + cw_ref[1:2, :] * u1 + cw_ref[2:3, :] * u
    y_b = gtb_ref[...] * conv

    merged = _sigmoid(ga_ref[...]) * y_a + _sigmoid(gb_ref[...]) * y_b
    x2_o[...] = x1_ref[...] + jnp.dot(merged.astype(BF16), wout_ref[...], preferred_element_type=F32)

    ut_o[...] = jnp.zeros_like(ut_o)
    for s in range(nseg):
        ut_o[2 * s:2 * s + 2, :] = u[(s + 1) * seg - 2:(s + 1) * seg, :]


def _post(y, v, bc, gl, p, x1, bgc, bxc, lng, lnb, cw, gup, wout, *, tm, seg):
    m, d = y.shape
    rowb = pl.BlockSpec((tm, d), lambda i: (i, 0))
    grp = lambda gi: pl.BlockSpec((tm, d), lambda i, gi=gi: (i, gi))
    vec = pl.BlockSpec((1, d), lambda i: (0, 0))
    bnd = pl.BlockSpec((SUBLANES, d), lambda i: (i, 0))
    return pl.pallas_call(
        functools.partial(_post_body, seg=seg),
        grid=(m // tm,),
        in_specs=[rowb, rowb, pl.BlockSpec((tm, LANES), lambda i: (i, 0)),
                  pl.BlockSpec((tm, LORA_G), lambda i: (i, 0)),
                  grp(3), grp(4), grp(5), grp(6), grp(7), rowb,
                  bnd, bnd, vec, vec,
                  pl.BlockSpec((CONV_W, d), lambda i: (0, 0)),
                  pl.BlockSpec((LORA_G, d), lambda i: (0, 0)),
                  pl.BlockSpec((d, d), lambda i: (0, 0), pipeline_mode=pl.Buffered(1))],
        out_specs=[rowb, bnd],
        out_shape=[jax.ShapeDtypeStruct((m, d), F32),
                   jax.ShapeDtypeStruct((m // tm * SUBLANES, d), F32)],
        compiler_params=_cparams(("parallel",)),
        name="post",
    )(y, v, bc, gl, p, p, p, p, p, x1, bgc, bxc, lng, lnb, cw, gup, wout)


def _hm(x):
    lead = x.shape[:-1]
    return x.reshape(*lead, N_HEADS, HEAD_DIM).swapaxes(-1, -2).reshape(*lead, D_MODEL)


def _hm_groups(x):
    lead = x.shape[:-1]
    n = x.shape[-1] // D_MODEL
    return x.reshape(*lead, n, N_HEADS, HEAD_DIM).swapaxes(-1, -2).reshape(*lead, n * D_MODEL)


def _hm_inv(x):
    lead = x.shape[:-1]
    return x.reshape(*lead, HEAD_DIM, N_HEADS).swapaxes(-1, -2).reshape(*lead, D_MODEL)


def _pad_lora(x):
    z = jnp.zeros(x.shape[:-1] + (LANES - LORA_W,), x.dtype)
    return jnp.concatenate([x[..., :LORA_W], z, x[..., LORA_W:LORA_W + LORA_A], z,
                            x[..., LORA_W + LORA_A:]], axis=-1)


def _pad_rows(x, rows):
    return jnp.concatenate([x, jnp.zeros((rows - x.shape[0],) + x.shape[1:], x.dtype)], axis=0)


def _tile_bounds(rows, n_tiles, n_used):
    w = rows.shape[-1]
    rows = rows.reshape(n_tiles, n_used, w)
    pad = jnp.zeros((n_tiles, SUBLANES - n_used, w), rows.dtype)
    return jnp.concatenate([rows, pad], axis=1).reshape(n_tiles * SUBLANES, w)


def _layer(x, shift_prev, conv_prev, wkv_prev, wts, *, tm_prep, tm_post, seg_prep, seg_post, tt):
    bsz, t_len, d = x.shape
    m = bsz * t_len
    xf = x.reshape(m, d)

    x1, h, h_last = _ffn(xf, wts["ffn1_norm"], wts["ffn1_gate"], wts["ffn1_up"], wts["ffn1_down"],
                         wts["mix_norm"], seg=min(t_len, FFN_TM))
    p = _proj(h, wts["w_groups"], tm=min(m, PROJ_TM))

    n_rkv = 3 * d
    n_tiles = m // tm_prep
    if shift_prev is None:
        starts = jnp.arange(n_tiles) * tm_prep
        prev = jnp.take(p, jnp.maximum(starts - 1, 0), axis=0)
        prev = jnp.where((starts % t_len == 0)[:, None], 0.0, prev)
        bnd = jnp.concatenate([prev[:, :n_rkv], prev[:, N_GROUPS * d:]], axis=1)
        n_used = 1
    else:
        rows = shift_prev.shape[0]
        ws = wts["w_groups"]
        bnd = _proj(_pad_rows(shift_prev, 32).astype(BF16), [ws[0], ws[2]], tm=32)[:rows]
        n_used = tm_prep // seg_prep
    bnd = _tile_bounds(bnd, n_tiles, n_used)

    r, w, k, v, kk, kka, gl, bc = _prep(p, bnd, wts["prep_vecs"], wts["prep_ups"],
                                       tm=tm_prep, seg=seg_prep)

    s0 = _state_to_tiles(wkv_prev)
    sh = (bsz, t_len, d)
    y, s_new = _scan(r.reshape(sh), w.reshape(sh), k.reshape(sh), v.reshape(sh),
                     kk.reshape(sh), kka.reshape(sh), s0, tt=tt)
    wkv_new = _tiles_to_state(s_new)

    n_tiles = m // tm_post
    if conv_prev is None:
        starts = jnp.arange(n_tiles) * tm_post
        idx = jnp.maximum(starts[:, None] + jnp.array([-2, -1])[None, :], 0).reshape(-1)
        rows = jnp.take(p, idx, axis=0)
        live = jnp.repeat(starts % t_len != 0, 2)[:, None]
        bgc = jnp.where(live, rows[:, 4 * d:5 * d], 0.0)
        bxc = rows[:, 5 * d:6 * d]
        n_used = 2
    else:
        bgc = _hm(conv_prev).reshape(bsz * (CONV_W - 1), d)
        bxc = jnp.ones_like(bgc)
        n_used = 2 * (tm_post // seg_post)
    bgc = _tile_bounds(bgc, n_tiles, n_used)
    bxc = _tile_bounds(bxc, n_tiles, n_used)

    x2, utail = _post(y.reshape(m, d), v, bc, gl, p, x1, bgc, bxc,
                      wts["ln_g"], wts["ln_b"], wts["conv_w"], wts["g_up"], wts["w_out"],
                      tm=tm_post, seg=seg_post)

    (yout,) = _ffn(x2, wts["ffn2_norm"], wts["ffn2_gate"], wts["ffn2_up"], wts["ffn2_down"],
                   wts["final_norm"], seg=None)

    segs = m // min(t_len, FFN_TM)
    h_last = h_last.reshape(m // FFN_TM, SUBLANES, d)[:, :segs // (m // FFN_TM)].reshape(segs, d)
    shift_new = h_last.reshape(bsz, segs // bsz, d)[:, -1]
    nseg_post = tm_post // seg_post
    ut = utail.reshape(n_tiles, SUBLANES, d)[:, :2 * nseg_post]
    ut = ut.reshape(n_tiles * nseg_post, 2, d)
    segs_per_seq = t_len // seg_post
    conv_new = _hm_inv(ut[segs_per_seq - 1::segs_per_seq])
    return yout.reshape(bsz, t_len, d), shift_new, conv_new, wkv_new


def kernel(x_prompt, x_sample, state_shift, state_conv, state_wkv, ffn1_norm, ffn1_gate, ffn1_up, ffn1_down, mix_norm, w_in, mu_shift, w0, w_lora_up, a0, a_lora_up, g_lora_up, k_k, k_a, r_k, ln_x_g, ln_x_b, conv_w, w_out, ffn2_norm, ffn2_gate, ffn2_up, ffn2_down, final_norm):
    depth = w_in.shape[0]
    assert depth == 1, "single-layer kernel"
    d = D_MODEL
    n_a = 3 * d + LORA_W + LORA_A + LORA_G
    l = 0

    wi = w_in[l]
    w_groups = [_hm_groups(wi[:, :3 * d].astype(BF16)), _hm_groups(wi[:, n_a:].astype(BF16)),
                _pad_lora(wi[:, 3 * d:n_a].astype(BF16))]

    mu = mu_shift[l]
    row = lambda x: x.reshape(1, -1)
    prep_vecs = (row(_hm(mu[0:d])), row(_hm(mu[d:2 * d])), row(_hm(mu[2 * d:3 * d])),
                 row(_pad_lora(mu[3 * d:])),
                 row(_hm(w0[l])), row(_hm(a0[l])), row(_hm(k_k[l])), row(_hm(k_a[l])),
                 row(_hm(r_k[l].reshape(d))))
    prep_ups = (_pad_rows(_hm(w_lora_up[l]), LANES).astype(BF16),
                _pad_rows(_hm(a_lora_up[l]), LANES).astype(BF16))

    wts = dict(
        ffn1_norm=row(ffn1_norm[l]), ffn1_gate=ffn1_gate[l].astype(BF16), ffn1_up=ffn1_up[l].astype(BF16),
        ffn1_down=ffn1_down[l].astype(BF16), mix_norm=row(mix_norm[l]), w_groups=w_groups,
        prep_vecs=prep_vecs, prep_ups=prep_ups, g_up=_hm(g_lora_up[l]).astype(BF16),
        ln_g=row(_hm(ln_x_g[l])), ln_b=row(_hm(ln_x_b[l])), conv_w=_hm(conv_w[l]),
        w_out=_hm(w_out[l].T).T.astype(BF16),
        ffn2_norm=row(ffn2_norm[l]), ffn2_gate=ffn2_gate[l].astype(BF16), ffn2_up=ffn2_up[l].astype(BF16),
        ffn2_down=ffn2_down[l].astype(BF16), final_norm=row(final_norm),
    )

    b_p, t_p, _ = x_prompt.shape
    b_s, t_s, _ = x_sample.shape
    zero_state = jnp.zeros((b_p, N_HEADS, HEAD_DIM, HEAD_DIM), state_wkv.dtype)
    y_p, sh_p, cv_p, wk_p = _layer(x_prompt, None, None, zero_state, wts,
                                   tm_prep=128, tm_post=256, seg_prep=128, seg_post=256, tt=min(t_p, 128))
    y_s, sh_s, cv_s, wk_s = _layer(x_sample, state_shift[l], state_conv[l], state_wkv[l], wts,
                                   tm_prep=128, tm_post=256, seg_prep=t_s, seg_post=t_s, tt=t_s)
    return (y_p, y_s, sh_p[None], cv_p[None], wk_p[None], sh_s[None], cv_s[None], wk_s[None])
```

```python
import functools
import math

import jax
import jax.numpy as jnp
from jax import lax
from jax.experimental import pallas as pl
from jax.experimental.pallas import tpu as pltpu

F32 = jnp.float32
BF16 = jnp.bfloat16

HEAD_DIM = 64
N_HEADS = 32
D_MODEL = N_HEADS * HEAD_DIM
CONV_W = 3
LORA_W = 96
LORA_A = 96
LORA_G = 256
RMS_EPS = 1e-6
GN_EPS = 64e-5
EXP_M05 = math.exp(-0.5)

LANES = 128
SUBLANES = 8
N_SLABS = D_MODEL // LANES
K_PER_SLAB = LANES // N_HEADS
LORA_PAD = 512
N_GROUPS = 8
VMEM_LIMIT = 56 * 1024 * 1024
FFN_TM = 512
FFN_TF = 512
PROJ_TM = 2048
PREP_TM = 128
POST_TM = 256
SCAN_TT = 128
SHIFT_ROWS = 32


def _cparams(sem):
    return pltpu.CompilerParams(dimension_semantics=sem, vmem_limit_bytes=VMEM_LIMIT)


def _sigmoid(x):
    return 1.0 / (1.0 + jnp.exp(-x))


def _ffn_body(x_ref, g1_ref, wg_ref, wu_ref, wd_ref, g2_ref, *rest, seg):
    if seg is None:
        ho_ref, xn_ref, acc_ref = rest
    else:
        xo_ref, ho_ref, hl_ref, xn_ref, acc_ref = rest
    j = pl.program_id(1)

    @pl.when(j == 0)
    def _():
        x = x_ref[...]
        ms = jnp.mean(x * x, axis=-1, keepdims=True)
        xn_ref[...] = (x * lax.rsqrt(ms + RMS_EPS) * g1_ref[...]).astype(BF16)
        acc_ref[...] = jnp.zeros_like(acc_ref)

    xn = xn_ref[...]
    gate = jnp.dot(xn, wg_ref[...], preferred_element_type=F32)
    up = jnp.dot(xn, wu_ref[...], preferred_element_type=F32)
    act = (gate * _sigmoid(gate) * up).astype(BF16)
    acc_ref[...] += jnp.dot(act, wd_ref[...], preferred_element_type=F32)

    @pl.when(j == pl.num_programs(1) - 1)
    def _():
        x1 = x_ref[...] + 0.5 * acc_ref[...]
        ms = jnp.mean(x1 * x1, axis=-1, keepdims=True)
        h = x1 * lax.rsqrt(ms + RMS_EPS) * g2_ref[...]
        if seg is None:
            ho_ref[...] = h
        else:
            xo_ref[...] = x1
            ho_ref[...] = h.astype(BF16)
            hl_ref[...] = jnp.zeros_like(hl_ref)
            for s in range(x1.shape[0] // seg):
                hl_ref[s:s + 1, :] = h[(s + 1) * seg - 1:(s + 1) * seg, :]


def _ffn(x, g1, wg, wu, wd, g2, *, seg, tm=FFN_TM, tf=FFN_TF):
    m, d = x.shape
    f = wg.shape[1]
    grid = (m // tm, f // tf)
    row = pl.BlockSpec((tm, d), lambda i, j: (i, 0))
    vec = pl.BlockSpec((1, d), lambda i, j: (0, 0))
    if seg is None:
        out_specs = [row]
        out_shape = [jax.ShapeDtypeStruct((m, d), F32)]
    else:
        assert tm // seg <= SUBLANES
        out_specs = [row, row, pl.BlockSpec((SUBLANES, d), lambda i, j: (i, 0))]
        out_shape = [jax.ShapeDtypeStruct((m, d), F32), jax.ShapeDtypeStruct((m, d), BF16),
                     jax.ShapeDtypeStruct((m // tm * SUBLANES, d), F32)]
    return pl.pallas_call(
        functools.partial(_ffn_body, seg=seg),
        grid=grid,
        in_specs=[row, vec,
                  pl.BlockSpec((d, tf), lambda i, j: (0, j)),
                  pl.BlockSpec((d, tf), lambda i, j: (0, j)),
                  pl.BlockSpec((tf, d), lambda i, j: (j, 0)),
                  vec],
        out_specs=out_specs,
        out_shape=out_shape,
        scratch_shapes=[pltpu.VMEM((tm, d), BF16), pltpu.VMEM((tm, d), F32)],
        compiler_params=_cparams(("parallel", "arbitrary")),
        name="ffn",
    )(x, g1, wg, wu, wd, g2)


def _proj_body(h_ref, *refs, starts):
    w_refs, o_ref = refs[:-1], refs[-1]
    j = pl.program_id(1)
    for g, w_ref in enumerate(w_refs):
        @pl.when((j >= starts[g]) & (j < starts[g + 1]))
        def _(w_ref=w_ref):
            o_ref[...] = jnp.dot(h_ref[...], w_ref[...], preferred_element_type=F32)


def _proj(h, ws, *, tm, tn=512):
    m, d = h.shape
    starts = [0]
    for w in ws:
        starts.append(starts[-1] + w.shape[1] // tn)
    w_specs = [pl.BlockSpec((d, tn), lambda i, j, lo=lo, hi=hi: (0, jnp.clip(j - lo, 0, hi - lo - 1)))
               for lo, hi in zip(starts[:-1], starts[1:])]
    return pl.pallas_call(
        functools.partial(_proj_body, starts=tuple(starts)),
        grid=(m // tm, starts[-1]),
        in_specs=[pl.BlockSpec((tm, d), lambda i, j: (i, 0))] + w_specs,
        out_specs=pl.BlockSpec((tm, tn), lambda i, j: (i, j)),
        out_shape=jax.ShapeDtypeStruct((m, starts[-1] * tn), F32),
        compiler_params=_cparams(("parallel", "arbitrary")),
        name="proj",
    )(h, *ws)


def _slab(x, q):
    return x[:, q * LANES:(q + 1) * LANES]


def _head_sum(x):
    acc = _slab(x, 0)
    for q in range(1, N_SLABS):
        acc = acc + _slab(x, q)
    acc = acc + pltpu.roll(acc, 2 * N_HEADS, axis=1)
    return acc + pltpu.roll(acc, N_HEADS, axis=1)


def _tile_slabs(x):
    return jnp.concatenate([x] * N_SLABS, axis=1)


def _prep_body(pr_ref, pk_ref, pv_ref, pl_ref, br_ref, bk_ref, bv_ref, bl_ref,
               mur_ref, muk_ref, muv_ref, mul_ref, w0_ref, a0_ref, kkp_ref, kap_ref, rkp_ref,
               wup_ref, aup_ref,
               r_o, w_o, k_o, v_o, kk_o, kka_o, gl_o, bc_o, *, seg):
    tm = pr_ref.shape[0]
    nseg = tm // seg
    row = lax.broadcasted_iota(jnp.int32, (tm, 1), 0)

    def shifted(p_ref, b_ref, mu_ref):
        p = p_ref[...]
        prev = pltpu.roll(p, 1, axis=0)
        for s in range(nseg):
            prev = jnp.where(row == s * seg, b_ref[s:s + 1, :], prev)
        return p + mu_ref[...] * (prev - p)

    r = shifted(pr_ref, br_ref, mur_ref)
    k = shifted(pk_ref, bk_ref, muk_ref)
    v = shifted(pv_ref, bv_ref, muv_ref)
    lo = shifted(pl_ref, bl_ref, mul_ref)

    wl = jnp.tanh(lo[:, 0:LANES]).astype(BF16)
    al = lo[:, LANES:2 * LANES].astype(BF16)
    z = w0_ref[...] + jnp.dot(wl, wup_ref[...], preferred_element_type=F32)
    decay = jnp.exp(-EXP_M05 * _sigmoid(z))
    a = _sigmoid(a0_ref[...] + jnp.dot(al, aup_ref[...], preferred_element_type=F32))

    kk = k * kkp_ref[...]
    norm = jnp.maximum(jnp.sqrt(_head_sum(kk * kk)), 1e-12)
    kk = kk * _tile_slabs(1.0 / norm)
    k_mod = k * (1.0 + (a - 1.0) * kap_ref[...])

    pos = row % SUBLANES
    q = decay
    for sh in (1, 2, 4):
        q = q * jnp.where(pos >= sh, pltpu.roll(q, sh, axis=0), 1.0)
    inv_q = 1.0 / q

    r_o[...] = r * q
    w_o[...] = q
    k_o[...] = k_mod * inv_q
    v_o[...] = v
    kk_o[...] = kk
    kka_o[...] = kk * a * inv_q
    gl_o[...] = _sigmoid(lo[:, 2 * LANES:LORA_PAD]).astype(BF16)
    bc_o[...] = _head_sum(r * k_mod * rkp_ref[...])


def _prep(p, bnd, vecs, ups, *, tm, seg):
    m = p.shape[0]
    d = D_MODEL
    lora_blk = N_GROUPS * d // LORA_PAD
    grp = lambda g: pl.BlockSpec((tm, d), lambda i, g=g: (i, g))
    bgrp = lambda g: pl.BlockSpec((SUBLANES, d), lambda i, g=g: (i, g))
    vec = pl.BlockSpec((1, d), lambda i: (0, 0))
    row_out = pl.BlockSpec((tm, d), lambda i: (i, 0))
    mur, muk, muv, mul, w0, a0, kkp, kap, rkp = vecs
    wup, aup = ups
    return pl.pallas_call(
        functools.partial(_prep_body, seg=seg),
        grid=(m // tm,),
        in_specs=[grp(0), grp(1), grp(2),
                  pl.BlockSpec((tm, LORA_PAD), lambda i: (i, lora_blk)),
                  bgrp(0), bgrp(1), bgrp(2),
                  pl.BlockSpec((SUBLANES, LORA_PAD), lambda i: (i, 3 * d // LORA_PAD)),
                  vec, vec, vec, pl.BlockSpec((1, LORA_PAD), lambda i: (0, 0)),
                  vec, vec, vec, vec, vec,
                  pl.BlockSpec((LANES, d), lambda i: (0, 0)),
                  pl.BlockSpec((LANES, d), lambda i: (0, 0))],
        out_specs=[row_out] * 6 + [pl.BlockSpec((tm, LORA_G), lambda i: (i, 0)),
                                   pl.BlockSpec((tm, LANES), lambda i: (i, 0))],
        out_shape=[jax.ShapeDtypeStruct((m, d), F32)] * 6 + [jax.ShapeDtypeStruct((m, LORA_G), BF16),
                                                             jax.ShapeDtypeStruct((m, LANES), F32)],
        compiler_params=_cparams(("parallel",)),
        name="prep",
    )(p, p, p, p, bnd, bnd, bnd, bnd, mur, muk, muv, mul, w0, a0, kkp, kap, rkp, wup, aup)


N_VT = HEAD_DIM // SUBLANES
N_KQ = N_SLABS
PAIR = 2


def _group_sum(x):
    x = x + pltpu.roll(x, 2 * N_HEADS, axis=1)
    return x + pltpu.roll(x, N_HEADS, axis=1)


def _scan_masks():
    sub = lax.broadcasted_iota(jnp.int32, (SUBLANES, LANES), 0)
    grp = lax.broadcasted_iota(jnp.int32, (SUBLANES, LANES), 1) // N_HEADS
    rot = (grp - sub % K_PER_SLAB) % K_PER_SLAB
    return rot, sub >= K_PER_SLAB, grp == sub % K_PER_SLAB


def _bcast_row(view, s, q):
    return jnp.broadcast_to(view[pl.ds(s, 1), pl.ds(q * LANES, LANES)], (SUBLANES, LANES))


def _scan_init(s0_ref, s_ref, sa_ref, kk_views):
    s_ref[...] = s0_ref[...]
    for b in range(PAIR):
        for j in range(N_VT):
            acc = s0_ref[b, j, 0] * _bcast_row(kk_views[b], 0, 0)
            for kq in range(1, N_KQ):
                acc = acc + s0_ref[b, j, kq] * _bcast_row(kk_views[b], 0, kq)
            sa_ref[b, j] = _group_sum(acc)


def _scan_step(b, views, s, js, s_ref, sa_ref, masks):
    rot, upper, diag = masks
    r_v, w_v, k_v, v_v, kk_v, kka_v, kkn_v, y_v = views
    vcols = {}
    for j in js:
        base = jnp.where(upper, _bcast_row(v_v, s, 2 * j + 1), _bcast_row(v_v, s, 2 * j))
        col = base
        for d in range(1, K_PER_SLAB):
            col = jnp.where(rot == d, pltpu.roll(base, d * N_HEADS, axis=1), col)
        vcols[j] = col
    sas = {j: sa_ref[b, j] for j in js}
    acc_y = {}
    acc_s = {}
    for kq in range(N_KQ):
        q_row = _bcast_row(w_v, s, kq)
        kka_row = _bcast_row(kka_v, s, kq)
        k_row = _bcast_row(k_v, s, kq)
        r_row = _bcast_row(r_v, s, kq)
        kkn_row = _bcast_row(kk_v, s + 1, kq) if s + 1 < SUBLANES else _bcast_row(kkn_v, 0, kq)
        qkn_row = q_row * kkn_row
        for j in js:
            s_new = s_ref[b, j, kq] - sas[j] * kka_row + vcols[j] * k_row
            s_ref[b, j, kq] = s_new * q_row if s == SUBLANES - 1 else s_new
            ty = s_new * r_row
            ts = s_new * qkn_row
            acc_y[j] = ty if kq == 0 else acc_y[j] + ty
            acc_s[j] = ts if kq == 0 else acc_s[j] + ts
    for j in js:
        sa_ref[b, j] = _group_sum(acc_s[j])
        yv = jnp.where(diag, _group_sum(acc_y[j]), 0.0)
        yv = yv + pltpu.roll(yv, 1, axis=0)
        yv = yv + pltpu.roll(yv, 2, axis=0)
        y_v[pl.ds(s, 1), pl.ds(2 * j * LANES, LANES)] = yv[K_PER_SLAB - 1:K_PER_SLAB, :]
        y_v[pl.ds(s, 1), pl.ds((2 * j + 1) * LANES, LANES)] = yv[SUBLANES - 1:SUBLANES, :]


_HALVES = (tuple(range(N_VT // 2)), tuple(range(N_VT // 2, N_VT)))


def _scan_steps8(views, s_ref, sa_ref, masks):
    for s in range(SUBLANES):
        for b in range(PAIR):
            for js in _HALVES:
                _scan_step(b, views[b], s, js, s_ref, sa_ref, masks)


def _scan_body(r_ref, w_ref, k_ref, v_ref, kk_ref, kka_ref, kkh_ref, s0_ref, y_ref, s_ref, sa_ref, *, tt):
    masks = _scan_masks()

    def tile8(ref, b, t8):
        start = t8 * SUBLANES
        if not isinstance(t8, int):
            start = pl.multiple_of(start, SUBLANES)
        return ref.at[b, pl.ds(start, SUBLANES), :]

    @pl.when(pl.program_id(1) == 0)
    def _():
        _scan_init(s0_ref, s_ref, sa_ref, [tile8(kk_ref, b, 0) for b in range(PAIR)])

    n8 = tt // SUBLANES

    def steps8(t8, last):
        views = []
        for b in range(PAIR):
            vb = tuple(tile8(ref, b, t8) for ref in (r_ref, w_ref, k_ref, v_ref, kk_ref, kka_ref))
            nxt = kkh_ref.at[b] if last else tile8(kk_ref, b, t8 + 1)
            views.append(vb + (nxt, tile8(y_ref, b, t8)))
        _scan_steps8(views, s_ref, sa_ref, masks)

    def loop_body(t8, carry):
        steps8(t8, False)
        return carry

    lax.fori_loop(0, n8 - 1, loop_body, 0)
    steps8(n8 - 1, True)


_STATE_BLOCK = (PAIR, N_VT, N_KQ, SUBLANES, LANES)


def _scan(r, w, k, v, kk, kka, s0, *, tt):
    bsz, t_len, d = r.shape
    seq = pl.BlockSpec((PAIR, tt, d), lambda i, j: (i, j, 0))
    st = pl.BlockSpec(_STATE_BLOCK, lambda i, j: (i, 0, 0, 0, 0))
    n8 = tt // SUBLANES
    last8 = t_len // SUBLANES - 1
    head = pl.BlockSpec((PAIR, SUBLANES, d), lambda i, j: (i, jnp.minimum((j + 1) * n8, last8), 0))
    return pl.pallas_call(
        functools.partial(_scan_body, tt=tt),
        grid=(bsz // PAIR, t_len // tt),
        in_specs=[seq] * 6 + [head, st],
        out_specs=[seq, st],
        out_shape=[jax.ShapeDtypeStruct((bsz, t_len, d), F32),
                   jax.ShapeDtypeStruct(s0.shape, F32)],
        scratch_shapes=[pltpu.VMEM((PAIR, N_VT, SUBLANES, LANES), F32)],
        compiler_params=_cparams(("parallel", "arbitrary")),
        name="wkv_scan",
    )(r, w, k, v, kk, kka, kk, s0)


def _ffn_scan_body(x_ref, g1_ref, wg_ref, wu_ref, wd_ref, g2_ref,
                   r_ref, w_ref, k_ref, v_ref, kk_ref, kka_ref, kkh_ref, s0_ref,
                   ho_ref, y_ref, s_ref, xn_ref, acc_ref, sa_ref, *, n_t8):
    j = pl.program_id(1)
    unit = pl.program_id(0) * pl.num_programs(1) + j

    @pl.when(j == 0)
    def _():
        x = x_ref[...]
        ms = jnp.mean(x * x, axis=-1, keepdims=True)
        xn_ref[...] = (x * lax.rsqrt(ms + RMS_EPS) * g1_ref[...]).astype(BF16)
        acc_ref[...] = jnp.zeros_like(acc_ref)

    @pl.when(unit % n_t8 == 0)
    def _():
        _scan_init(s0_ref, s_ref, sa_ref, [kk_ref.at[b] for b in range(PAIR)])

    xn = xn_ref[...]
    gate = jnp.dot(xn, wg_ref[...], preferred_element_type=F32)
    up = jnp.dot(xn, wu_ref[...], preferred_element_type=F32)
    act = (gate * _sigmoid(gate) * up).astype(BF16)
    acc_ref[...] += jnp.dot(act, wd_ref[...], preferred_element_type=F32)

    views = [tuple(ref.at[b] for ref in (r_ref, w_ref, k_ref, v_ref, kk_ref, kka_ref, kkh_ref, y_ref))
             for b in range(PAIR)]
    _scan_steps8(views, s_ref, sa_ref, _scan_masks())

    @pl.when(j == pl.num_programs(1) - 1)
    def _():
        x1 = x_ref[...] + 0.5 * acc_ref[...]
        ms = jnp.mean(x1 * x1, axis=-1, keepdims=True)
        ho_ref[...] = x1 * lax.rsqrt(ms + RMS_EPS) * g2_ref[...]


def _ffn_scan(x, g1, wg, wu, wd, g2, r, w, k, v, kk, kka, s0, *, tm=FFN_TM, tf=FFN_TF):
    m, d = x.shape
    n_f = wg.shape[1] // tf
    bsz, t_len, _ = r.shape
    n_t8 = t_len // SUBLANES
    n_pairs = bsz // PAIR
    steps = (m // tm) * n_f
    assert steps >= n_pairs * n_t8
    pad_pairs = -(-steps // n_t8)
    unit = lambda i, j: i * n_f + j
    pair_in = lambda i, j: jnp.minimum(unit(i, j) // n_t8, n_pairs - 1)
    row = pl.BlockSpec((tm, d), lambda i, j: (i, 0))
    vec = pl.BlockSpec((1, d), lambda i, j: (0, 0))
    tile = (PAIR, SUBLANES, d)
    seq_in = pl.BlockSpec(tile, lambda i, j: (pair_in(i, j), unit(i, j) % n_t8, 0))
    head = pl.BlockSpec(tile, lambda i, j: (pair_in(i, j), jnp.minimum(unit(i, j) % n_t8 + 1, n_t8 - 1), 0))
    st_in = pl.BlockSpec(_STATE_BLOCK, lambda i, j: (pair_in(i, j), 0, 0, 0, 0))
    seq_out = pl.BlockSpec(tile, lambda i, j: (unit(i, j) // n_t8, unit(i, j) % n_t8, 0))
    st_out = pl.BlockSpec(_STATE_BLOCK, lambda i, j: (unit(i, j) // n_t8, 0, 0, 0, 0))
    ho, y, s_new = pl.pallas_call(
        functools.partial(_ffn_scan_body, n_t8=n_t8),
        grid=(m // tm, n_f),
        in_specs=[row, vec,
                  pl.BlockSpec((d, tf), lambda i, j: (0, j)),
                  pl.BlockSpec((d, tf), lambda i, j: (0, j)),
                  pl.BlockSpec((tf, d), lambda i, j: (j, 0)),
                  vec] + [seq_in] * 6 + [head, st_in],
        out_specs=[row, seq_out, st_out],
        out_shape=[jax.ShapeDtypeStruct((m, d), F32),
                   jax.ShapeDtypeStruct((pad_pairs * PAIR, t_len, d), F32),
                   jax.ShapeDtypeStruct((pad_pairs * PAIR,) + s0.shape[1:], F32)],
        scratch_shapes=[pltpu.VMEM((tm, d), BF16), pltpu.VMEM((tm, d), F32),
                        pltpu.VMEM((PAIR, N_VT, SUBLANES, LANES), F32)],
        compiler_params=_cparams(("arbitrary", "arbitrary")),
        name="ffn_scan",
    )(x, g1, wg, wu, wd, g2, r, w, k, v, kk, kka, kk, s0)
    return ho, y[:bsz], s_new[:bsz]


def _state_to_tiles(s):
    b = s.shape[0]
    s = s.reshape(b, N_HEADS, N_VT, SUBLANES, N_KQ, K_PER_SLAB)
    return s.transpose(0, 2, 4, 3, 5, 1).reshape(b, N_VT, N_KQ, SUBLANES, LANES)


def _tiles_to_state(s):
    b = s.shape[0]
    s = s.reshape(b, N_VT, N_KQ, SUBLANES, K_PER_SLAB, N_HEADS)
    return s.transpose(0, 5, 1, 3, 2, 4).reshape(b, N_HEADS, HEAD_DIM, HEAD_DIM)


def _post_body(y_ref, v_ref, bc_ref, gl_ref, gtb_ref, gtc_ref, xc_ref, ga_ref, gb_ref, x1_ref,
               bgc_ref, bxc_ref, lng_ref, lnb_ref, cw_ref, gup_ref, wout_ref, x2_o, ut_o, *, seg):
    tm = y_ref.shape[0]
    nseg = tm // seg
    row = lax.broadcasted_iota(jnp.int32, (tm, 1), 0)
    inv_n = 1.0 / HEAD_DIM

    o = y_ref[...]
    mu = _head_sum(o) * inv_n
    c = o - _tile_slabs(mu)
    var = _head_sum(c * c) * inv_n
    o = c * _tile_slabs(lax.rsqrt(var + GN_EPS)) * lng_ref[...] + lnb_ref[...]
    bonus = _tile_slabs(bc_ref[...]) * v_ref[...]
    g = jnp.dot(gl_ref[...], gup_ref[...], preferred_element_type=F32)
    y_a = (o + bonus) * g

    u = gtc_ref[...] * xc_ref[...]
    ub = bgc_ref[...] * bxc_ref[...]
    u1 = pltpu.roll(u, 1, axis=0)
    u2 = pltpu.roll(u, 2, axis=0)
    for s in range(nseg):
        u1 = jnp.where(row == s * seg, ub[2 * s + 1:2 * s + 2, :], u1)
        u2 = jnp.where(row == s * seg, ub[2 * s:2 * s + 1, :], u2)
        u2 = jnp.where(row == s * seg + 1, ub[2 * s + 1:2 * s + 2, :], u2)
    conv = cw_ref[0:1, :] * u2 + cw_ref[1:2, :] * u1 + cw_ref[2:3, :] * u
    y_b = gtb_ref[...] * conv

    merged = _sigmoid(ga_ref[...]) * y_a + _sigmoid(gb_ref[...]) * y_b
    x2_o[...] = x1_ref[...] + jnp.dot(merged.astype(BF16), wout_ref[...], preferred_element_type=F32)

    ut_o[...] = jnp.zeros_like(ut_o)
    for s in range(nseg):
        ut_o[2 * s:2 * s + 2, :] = u[(s + 1) * seg - 2:(s + 1) * seg, :]


def _post(y, v, bc, gl, p, x1, bgc, bxc, lng, lnb, cw, gup, wout, *, tm, seg):
    m, d = y.shape
    rowb = pl.BlockSpec((tm, d), lambda i: (i, 0))
    grp = lambda gi: pl.BlockSpec((tm, d), lambda i, gi=gi: (i, gi))
    vec = pl.BlockSpec((1, d), lambda i: (0, 0))
    bnd = pl.BlockSpec((SUBLANES, d), lambda i: (i, 0))
    return pl.pallas_call(
        functools.partial(_post_body, seg=seg),
        grid=(m // tm,),
        in_specs=[rowb, rowb, pl.BlockSpec((tm, LANES), lambda i: (i, 0)),
                  pl.BlockSpec((tm, LORA_G), lambda i: (i, 0)),
                  grp(3), grp(4), grp(5), grp(6), grp(7), rowb,
                  bnd, bnd, vec, vec,
                  pl.BlockSpec((CONV_W, d), lambda i: (0, 0)),
                  pl.BlockSpec((LORA_G, d), lambda i: (0, 0)),
                  pl.BlockSpec((d, d), lambda i: (0, 0), pipeline_mode=pl.Buffered(1))],
        out_specs=[rowb, bnd],
        out_shape=[jax.ShapeDtypeStruct((m, d), F32),
                   jax.ShapeDtypeStruct((m // tm * SUBLANES, d), F32)],
        compiler_params=_cparams(("parallel",)),
        name="post",
    )(y, v, bc, gl, p, p, p, p, p, x1, bgc, bxc, lng, lnb, cw, gup, wout)


def _hm(x):
    lead = x.shape[:-1]
    return x.reshape(*lead, N_HEADS, HEAD_DIM).swapaxes(-1, -2).reshape(*lead, D_MODEL)


def _hm_groups(x):
    lead = x.shape[:-1]
    n = x.shape[-1] // D_MODEL
    return x.reshape(*lead, n, N_HEADS, HEAD_DIM).swapaxes(-1, -2).reshape(*lead, n * D_MODEL)


def _hm_inv(x):
    lead = x.shape[:-1]
    return x.reshape(*lead, HEAD_DIM, N_HEADS).swapaxes(-1, -2).reshape(*lead, D_MODEL)


def _pad_lora(x):
    z = jnp.zeros(x.shape[:-1] + (LANES - LORA_W,), x.dtype)
    return jnp.concatenate([x[..., :LORA_W], z, x[..., LORA_W:LORA_W + LORA_A], z,
                            x[..., LORA_W + LORA_A:]], axis=-1)


def _pad_rows(x, rows):
    return jnp.concatenate([x, jnp.zeros((rows - x.shape[0],) + x.shape[1:], x.dtype)], axis=0)


def _tile_bounds(rows, n_tiles, n_used):
    w = rows.shape[-1]
    rows = rows.reshape(n_tiles, n_used, w)
    pad = jnp.zeros((n_tiles, SUBLANES - n_used, w), rows.dtype)
    return jnp.concatenate([rows, pad], axis=1).reshape(n_tiles * SUBLANES, w)


def _front(x, shift_prev, wkv_prev, wts, *, tm_prep, seg_prep):
    bsz, t_len, d = x.shape
    m = bsz * t_len
    x1, h, h_last = _ffn(x.reshape(m, d), wts["ffn1_norm"], wts["ffn1_gate"], wts["ffn1_up"], wts["ffn1_down"],
                         wts["mix_norm"], seg=min(t_len, FFN_TM))
    p = _proj(h, wts["w_groups"], tm=min(m, PROJ_TM))

    n_tiles = m // tm_prep
    if shift_prev is None:
        starts = jnp.arange(n_tiles) * tm_prep
        prev = jnp.take(p, jnp.maximum(starts - 1, 0), axis=0)
        prev = jnp.where((starts % t_len == 0)[:, None], 0.0, prev)
        bnd = jnp.concatenate([prev[:, :3 * d], prev[:, N_GROUPS * d:]], axis=1)
        n_used = 1
    else:
        rows = shift_prev.shape[0]
        ws = wts["w_groups"]
        bnd = _proj(_pad_rows(shift_prev, SHIFT_ROWS).astype(BF16), [ws[0], ws[2]], tm=SHIFT_ROWS)[:rows]
        n_used = tm_prep // seg_prep
    bnd = _tile_bounds(bnd, n_tiles, n_used)

    r, w, k, v, kk, kka, gl, bc = _prep(p, bnd, wts["prep_vecs"], wts["prep_ups"], tm=tm_prep, seg=seg_prep)
    sh = (bsz, t_len, d)
    scan_in = tuple(a.reshape(sh) for a in (r, w, k, v, kk, kka)) + (_state_to_tiles(wkv_prev),)

    segs = m // min(t_len, FFN_TM)
    h_last = h_last.reshape(m // FFN_TM, SUBLANES, d)[:, :segs // (m // FFN_TM)].reshape(segs, d)
    shift_new = h_last.reshape(bsz, segs // bsz, d)[:, -1]
    return dict(x1=x1, p=p, v=v, gl=gl, bc=bc, scan_in=scan_in, shift_new=shift_new, shape=sh)


def _merge(fr, y, conv_prev, wts, *, tm_post, seg_post):
    bsz, t_len, d = fr["shape"]
    m = bsz * t_len
    p = fr["p"]
    n_tiles = m // tm_post
    if conv_prev is None:
        starts = jnp.arange(n_tiles) * tm_post
        idx = jnp.maximum(starts[:, None] + jnp.array([-2, -1])[None, :], 0).reshape(-1)
        rows = jnp.take(p, idx, axis=0)
        live = jnp.repeat(starts % t_len != 0, 2)[:, None]
        bgc = jnp.where(live, rows[:, 4 * d:5 * d], 0.0)
        bxc = rows[:, 5 * d:6 * d]
        n_used = 2
    else:
        bgc = _hm(conv_prev).reshape(bsz * (CONV_W - 1), d)
        bxc = jnp.ones_like(bgc)
        n_used = 2 * (tm_post // seg_post)
    bgc = _tile_bounds(bgc, n_tiles, n_used)
    bxc = _tile_bounds(bxc, n_tiles, n_used)

    x2, utail = _post(y.reshape(m, d), fr["v"], fr["bc"], fr["gl"], p, fr["x1"], bgc, bxc,
                      wts["ln_g"], wts["ln_b"], wts["conv_w"], wts["g_up"], wts["w_out"],
                      tm=tm_post, seg=seg_post)
    nseg_post = tm_post // seg_post
    ut = utail.reshape(n_tiles, SUBLANES, d)[:, :2 * nseg_post]
    ut = ut.reshape(n_tiles * nseg_post, 2, d)
    segs_per_seq = t_len // seg_post
    return x2, _hm_inv(ut[segs_per_seq - 1::segs_per_seq])


def kernel(x_prompt, x_sample, state_shift, state_conv, state_wkv, ffn1_norm, ffn1_gate, ffn1_up, ffn1_down, mix_norm, w_in, mu_shift, w0, w_lora_up, a0, a_lora_up, g_lora_up, k_k, k_a, r_k, ln_x_g, ln_x_b, conv_w, w_out, ffn2_norm, ffn2_gate, ffn2_up, ffn2_down, final_norm):
    depth = w_in.shape[0]
    assert depth == 1, "single-layer kernel"
    d = D_MODEL
    n_a = 3 * d + LORA_W + LORA_A + LORA_G
    l = 0

    wi = w_in[l]
    w_groups = [_hm_groups(wi[:, :3 * d].astype(BF16)), _hm_groups(wi[:, n_a:].astype(BF16)),
                _pad_lora(wi[:, 3 * d:n_a].astype(BF16))]

    mu = mu_shift[l]
    row = lambda x: x.reshape(1, -1)
    prep_vecs = (row(_hm(mu[0:d])), row(_hm(mu[d:2 * d])), row(_hm(mu[2 * d:3 * d])),
                 row(_pad_lora(mu[3 * d:])),
                 row(_hm(w0[l])), row(_hm(a0[l])), row(_hm(k_k[l])), row(_hm(k_a[l])),
                 row(_hm(r_k[l].reshape(d))))
    prep_ups = (_pad_rows(_hm(w_lora_up[l]), LANES).astype(BF16),
                _pad_rows(_hm(a_lora_up[l]), LANES).astype(BF16))

    wts = dict(
        ffn1_norm=row(ffn1_norm[l]), ffn1_gate=ffn1_gate[l].astype(BF16), ffn1_up=ffn1_up[l].astype(BF16),
        ffn1_down=ffn1_down[l].astype(BF16), mix_norm=row(mix_norm[l]), w_groups=w_groups,
        prep_vecs=prep_vecs, prep_ups=prep_ups, g_up=_hm(g_lora_up[l]).astype(BF16),
        ln_g=row(_hm(ln_x_g[l])), ln_b=row(_hm(ln_x_b[l])), conv_w=_hm(conv_w[l]),
        w_out=_hm(w_out[l].T).T.astype(BF16),
        ffn2_norm=row(ffn2_norm[l]), ffn2_gate=ffn2_gate[l].astype(BF16), ffn2_up=ffn2_up[l].astype(BF16),
        ffn2_down=ffn2_down[l].astype(BF16), final_norm=row(final_norm),
    )

    b_p, t_p, _ = x_prompt.shape
    b_s, t_s, _ = x_sample.shape
    ffn2 = (wts["ffn2_norm"], wts["ffn2_gate"], wts["ffn2_up"], wts["ffn2_down"], wts["final_norm"])
    zero_state = jnp.zeros((b_p, N_HEADS, HEAD_DIM, HEAD_DIM), state_wkv.dtype)

    fp = _front(x_prompt, None, zero_state, wts, tm_prep=PREP_TM, seg_prep=PREP_TM)
    y_p, s_p = _scan(*fp["scan_in"], tt=min(t_p, SCAN_TT))
    x2_p, cv_p = _merge(fp, y_p, None, wts, tm_post=POST_TM, seg_post=POST_TM)

    fs = _front(x_sample, state_shift[l], state_wkv[l], wts, tm_prep=PREP_TM, seg_prep=t_s)
    ffn_steps = (b_p * t_p // FFN_TM) * (ffn2[1].shape[1] // FFN_TF)
    if ffn_steps >= (b_s // PAIR) * (t_s // SUBLANES):
        yo_p, y_s, s_s = _ffn_scan(x2_p, *ffn2, *fs["scan_in"])
    else:
        (yo_p,) = _ffn(x2_p, *ffn2, seg=None)
        y_s, s_s = _scan(*fs["scan_in"], tt=t_s)
    x2_s, cv_s = _merge(fs, y_s, state_conv[l], wts, tm_post=POST_TM, seg_post=t_s)
    (yo_s,) = _ffn(x2_s, *ffn2, seg=None)

    return (yo_p.reshape(x_prompt.shape), yo_s.reshape(x_sample.shape),
            fp["shift_new"][None], cv_p[None], _tiles_to_state(s_p)[None],
            fs["shift_new"][None], cv_s[None], _tiles_to_state(s_s)[None])
```

```python
import functools
import math

import jax
import jax.numpy as jnp
from jax import lax
from jax.experimental import pallas as pl
from jax.experimental.pallas import tpu as pltpu

F32 = jnp.float32
BF16 = jnp.bfloat16

HEAD_DIM = 64
N_HEADS = 32
D_MODEL = N_HEADS * HEAD_DIM
CONV_W = 3
LORA_W = 96
LORA_A = 96
LORA_G = 256
RMS_EPS = 1e-6
GN_EPS = 64e-5
EXP_M05 = math.exp(-0.5)

LANES = 128
SUBLANES = 8
N_SLABS = D_MODEL // LANES
K_PER_SLAB = LANES // N_HEADS
LORA_PAD = 512
N_GROUPS = 8
VMEM_LIMIT = 56 * 1024 * 1024
FFN_TM = 512
FFN_TF = 512
PROJ_TM = 2048
PREP_TM = 128
POST_TM = 256
SCAN_TT = 128
SHIFT_ROWS = 32


def _cparams(sem):
    return pltpu.CompilerParams(dimension_semantics=sem, vmem_limit_bytes=VMEM_LIMIT)


def _sigmoid(x):
    return 1.0 / (1.0 + jnp.exp(-x))


def _ffn_body(x_ref, g1_ref, wg_ref, wu_ref, wd_ref, g2_ref, *rest, seg):
    if seg is None:
        ho_ref, xn_ref, acc_ref = rest
    else:
        xo_ref, ho_ref, hl_ref, xn_ref, acc_ref = rest
    j = pl.program_id(1)

    @pl.when(j == 0)
    def _():
        x = x_ref[...]
        ms = jnp.mean(x * x, axis=-1, keepdims=True)
        xn_ref[...] = (x * lax.rsqrt(ms + RMS_EPS) * g1_ref[...]).astype(BF16)
        acc_ref[...] = jnp.zeros_like(acc_ref)

    xn = xn_ref[...]
    gate = jnp.dot(xn, wg_ref[...], preferred_element_type=F32)
    up = jnp.dot(xn, wu_ref[...], preferred_element_type=F32)
    act = (gate * _sigmoid(gate) * up).astype(BF16)
    acc_ref[...] += jnp.dot(act, wd_ref[...], preferred_element_type=F32)

    @pl.when(j == pl.num_programs(1) - 1)
    def _():
        x1 = x_ref[...] + 0.5 * acc_ref[...]
        ms = jnp.mean(x1 * x1, axis=-1, keepdims=True)
        h = x1 * lax.rsqrt(ms + RMS_EPS) * g2_ref[...]
        if seg is None:
            ho_ref[...] = h
        else:
            xo_ref[...] = x1
            ho_ref[...] = h.astype(BF16)
            hl_ref[...] = jnp.zeros_like(hl_ref)
            for s in range(x1.shape[0] // seg):
                hl_ref[s:s + 1, :] = h[(s + 1) * seg - 1:(s + 1) * seg, :]


def _ffn(x, g1, wg, wu, wd, g2, *, seg, tm=FFN_TM, tf=FFN_TF):
    m, d = x.shape
    f = wg.shape[1]
    grid = (m // tm, f // tf)
    row = pl.BlockSpec((tm, d), lambda i, j: (i, 0))
    vec = pl.BlockSpec((1, d), lambda i, j: (0, 0))
    if seg is None:
        out_specs = [row]
        out_shape = [jax.ShapeDtypeStruct((m, d), F32)]
    else:
        assert tm // seg <= SUBLANES
        out_specs = [row, row, pl.BlockSpec((SUBLANES, d), lambda i, j: (i, 0))]
        out_shape = [jax.ShapeDtypeStruct((m, d), F32), jax.ShapeDtypeStruct((m, d), BF16),
                     jax.ShapeDtypeStruct((m // tm * SUBLANES, d), F32)]
    return pl.pallas_call(
        functools.partial(_ffn_body, seg=seg),
        grid=grid,
        in_specs=[row, vec,
                  pl.BlockSpec((d, tf), lambda i, j: (0, j)),
                  pl.BlockSpec((d, tf), lambda i, j: (0, j)),
                  pl.BlockSpec((tf, d), lambda i, j: (j, 0)),
                  vec],
        out_specs=out_specs,
        out_shape=out_shape,
        scratch_shapes=[pltpu.VMEM((tm, d), BF16), pltpu.VMEM((tm, d), F32)],
        compiler_params=_cparams(("parallel", "arbitrary")),
        name="ffn",
    )(x, g1, wg, wu, wd, g2)


def _proj_body(h_ref, *refs, starts):
    w_refs, o_ref = refs[:-1], refs[-1]
    j = pl.program_id(1)
    for g, w_ref in enumerate(w_refs):
        @pl.when((j >= starts[g]) & (j < starts[g + 1]))
        def _(w_ref=w_ref):
            o_ref[...] = jnp.dot(h_ref[...], w_ref[...], preferred_element_type=F32)


def _proj(h, ws, *, tm, tn=512):
    m, d = h.shape
    starts = [0]
    for w in ws:
        starts.append(starts[-1] + w.shape[1] // tn)
    w_specs = [pl.BlockSpec((d, tn), lambda i, j, lo=lo, hi=hi: (0, jnp.clip(j - lo, 0, hi - lo - 1)))
               for lo, hi in zip(starts[:-1], starts[1:])]
    return pl.pallas_call(
        functools.partial(_proj_body, starts=tuple(starts)),
        grid=(m // tm, starts[-1]),
        in_specs=[pl.BlockSpec((tm, d), lambda i, j: (i, 0))] + w_specs,
        out_specs=pl.BlockSpec((tm, tn), lambda i, j: (i, j)),
        out_shape=jax.ShapeDtypeStruct((m, starts[-1] * tn), F32),
        compiler_params=_cparams(("parallel", "arbitrary")),
        name="proj",
    )(h, *ws)


def _slab(x, q):
    return x[:, q * LANES:(q + 1) * LANES]


def _head_sum(x):
    acc = _slab(x, 0)
    for q in range(1, N_SLABS):
        acc = acc + _slab(x, q)
    acc = acc + pltpu.roll(acc, 2 * N_HEADS, axis=1)
    return acc + pltpu.roll(acc, N_HEADS, axis=1)


def _tile_slabs(x):
    return jnp.concatenate([x] * N_SLABS, axis=1)


def _prep_body(pr_ref, pk_ref, pv_ref, pl_ref, br_ref, bk_ref, bv_ref, bl_ref,
               mur_ref, muk_ref, muv_ref, mul_ref, w0_ref, a0_ref, kkp_ref, kap_ref, rkp_ref,
               wup_ref, aup_ref,
               r_o, w_o, k_o, v_o, kk_o, kka_o, gl_o, bc_o, *, seg):
    tm = pr_ref.shape[0]
    nseg = tm // seg
    row = lax.broadcasted_iota(jnp.int32, (tm, 1), 0)

    def shifted(p_ref, b_ref, mu_ref):
        p = p_ref[...]
        prev = pltpu.roll(p, 1, axis=0)
        for s in range(nseg):
            prev = jnp.where(row == s * seg, b_ref[s:s + 1, :], prev)
        return p + mu_ref[...] * (prev - p)

    r = shifted(pr_ref, br_ref, mur_ref)
    k = shifted(pk_ref, bk_ref, muk_ref)
    v = shifted(pv_ref, bv_ref, muv_ref)
    lo = shifted(pl_ref, bl_ref, mul_ref)

    wl = jnp.tanh(lo[:, 0:LANES]).astype(BF16)
    al = lo[:, LANES:2 * LANES].astype(BF16)
    z = w0_ref[...] + jnp.dot(wl, wup_ref[...], preferred_element_type=F32)
    decay = jnp.exp(-EXP_M05 * _sigmoid(z))
    a = _sigmoid(a0_ref[...] + jnp.dot(al, aup_ref[...], preferred_element_type=F32))

    kk = k * kkp_ref[...]
    norm = jnp.maximum(jnp.sqrt(_head_sum(kk * kk)), 1e-12)
    kk = kk * _tile_slabs(1.0 / norm)
    k_mod = k * (1.0 + (a - 1.0) * kap_ref[...])

    pos = row % SUBLANES
    q = decay
    for sh in (1, 2, 4):
        q = q * jnp.where(pos >= sh, pltpu.roll(q, sh, axis=0), 1.0)
    inv_q = 1.0 / q

    r_o[...] = r * q
    w_o[...] = q
    k_o[...] = k_mod * inv_q
    v_o[...] = v
    kk_o[...] = kk
    kka_o[...] = kk * a * inv_q
    gl_o[...] = _sigmoid(lo[:, 2 * LANES:LORA_PAD]).astype(BF16)
    bc_o[...] = _head_sum(r * k_mod * rkp_ref[...])


def _prep(p, bnd, vecs, ups, *, tm, seg):
    m = p.shape[0]
    d = D_MODEL
    lora_blk = N_GROUPS * d // LORA_PAD
    grp = lambda g: pl.BlockSpec((tm, d), lambda i, g=g: (i, g))
    bgrp = lambda g: pl.BlockSpec((SUBLANES, d), lambda i, g=g: (i, g))
    vec = pl.BlockSpec((1, d), lambda i: (0, 0))
    row_out = pl.BlockSpec((tm, d), lambda i: (i, 0))
    mur, muk, muv, mul, w0, a0, kkp, kap, rkp = vecs
    wup, aup = ups
    return pl.pallas_call(
        functools.partial(_prep_body, seg=seg),
        grid=(m // tm,),
        in_specs=[grp(0), grp(1), grp(2),
                  pl.BlockSpec((tm, LORA_PAD), lambda i: (i, lora_blk)),
                  bgrp(0), bgrp(1), bgrp(2),
                  pl.BlockSpec((SUBLANES, LORA_PAD), lambda i: (i, 3 * d // LORA_PAD)),
                  vec, vec, vec, pl.BlockSpec((1, LORA_PAD), lambda i: (0, 0)),
                  vec, vec, vec, vec, vec,
                  pl.BlockSpec((LANES, d), lambda i: (0, 0)),
                  pl.BlockSpec((LANES, d), lambda i: (0, 0))],
        out_specs=[row_out] * 6 + [pl.BlockSpec((tm, LORA_G), lambda i: (i, 0)),
                                   pl.BlockSpec((tm, LANES), lambda i: (i, 0))],
        out_shape=[jax.ShapeDtypeStruct((m, d), F32)] * 6 + [jax.ShapeDtypeStruct((m, LORA_G), BF16),
                                                             jax.ShapeDtypeStruct((m, LANES), F32)],
        compiler_params=_cparams(("parallel",)),
        name="prep",
    )(p, p, p, p, bnd, bnd, bnd, bnd, mur, muk, muv, mul, w0, a0, kkp, kap, rkp, wup, aup)


N_VT = HEAD_DIM // SUBLANES
N_KQ = N_SLABS
PAIR = 2


def _group_sum(x):
    x = x + pltpu.roll(x, 2 * N_HEADS, axis=1)
    return x + pltpu.roll(x, N_HEADS, axis=1)


def _scan_masks():
    sub = lax.broadcasted_iota(jnp.int32, (SUBLANES, LANES), 0)
    grp = lax.broadcasted_iota(jnp.int32, (SUBLANES, LANES), 1) // N_HEADS
    rot = (grp - sub % K_PER_SLAB) % K_PER_SLAB
    return rot, sub >= K_PER_SLAB, grp == sub % K_PER_SLAB


def _bcast_row(view, s, q):
    return jnp.broadcast_to(view[pl.ds(s, 1), pl.ds(q * LANES, LANES)], (SUBLANES, LANES))


def _scan_init(s0_ref, s_ref, sa_ref, kk_views):
    s_ref[...] = s0_ref[...]
    for b in range(PAIR):
        for j in range(N_VT):
            acc = s0_ref[b, j, 0] * _bcast_row(kk_views[b], 0, 0)
            for kq in range(1, N_KQ):
                acc = acc + s0_ref[b, j, kq] * _bcast_row(kk_views[b], 0, kq)
            sa_ref[b, j] = _group_sum(acc)


def _scan_step(b, views, s, js, s_ref, sa_ref, masks):
    rot, upper, diag = masks
    r_v, w_v, k_v, v_v, kk_v, kka_v, kkn_v, y_v = views
    vcols = {}
    for j in js:
        base = jnp.where(upper, _bcast_row(v_v, s, 2 * j + 1), _bcast_row(v_v, s, 2 * j))
        col = base
        for d in range(1, K_PER_SLAB):
            col = jnp.where(rot == d, pltpu.roll(base, d * N_HEADS, axis=1), col)
        vcols[j] = col
    sas = {j: sa_ref[b, j] for j in js}
    acc_y = {}
    acc_s = {}
    for kq in range(N_KQ):
        q_row = _bcast_row(w_v, s, kq)
        kka_row = _bcast_row(kka_v, s, kq)
        k_row = _bcast_row(k_v, s, kq)
        r_row = _bcast_row(r_v, s, kq)
        kkn_row = _bcast_row(kk_v, s + 1, kq) if s + 1 < SUBLANES else _bcast_row(kkn_v, 0, kq)
        qkn_row = q_row * kkn_row
        for j in js:
            s_new = s_ref[b, j, kq] - sas[j] * kka_row + vcols[j] * k_row
            s_ref[b, j, kq] = s_new * q_row if s == SUBLANES - 1 else s_new
            ty = s_new * r_row
            ts = s_new * qkn_row
            acc_y[j] = ty if kq == 0 else acc_y[j] + ty
            acc_s[j] = ts if kq == 0 else acc_s[j] + ts
    for j in js:
        sa_ref[b, j] = _group_sum(acc_s[j])
        yv = jnp.where(diag, _group_sum(acc_y[j]), 0.0)
        yv = yv + pltpu.roll(yv, 1, axis=0)
        yv = yv + pltpu.roll(yv, 2, axis=0)
        y_v[pl.ds(s, 1), pl.ds(2 * j * LANES, LANES)] = yv[K_PER_SLAB - 1:K_PER_SLAB, :]
        y_v[pl.ds(s, 1), pl.ds((2 * j + 1) * LANES, LANES)] = yv[SUBLANES - 1:SUBLANES, :]


_HALVES = (tuple(range(N_VT // 2)), tuple(range(N_VT // 2, N_VT)))


def _scan_steps8(views, s_ref, sa_ref, masks):
    for s in range(SUBLANES):
        for b in range(PAIR):
            for js in _HALVES:
                _scan_step(b, views[b], s, js, s_ref, sa_ref, masks)


def _scan_body(r_ref, w_ref, k_ref, v_ref, kk_ref, kka_ref, kkh_ref, s0_ref, y_ref, s_ref, sa_ref, *, tt):
    masks = _scan_masks()

    def tile8(ref, b, t8):
        start = t8 * SUBLANES
        if not isinstance(t8, int):
            start = pl.multiple_of(start, SUBLANES)
        return ref.at[b, pl.ds(start, SUBLANES), :]

    @pl.when(pl.program_id(1) == 0)
    def _():
        _scan_init(s0_ref, s_ref, sa_ref, [tile8(kk_ref, b, 0) for b in range(PAIR)])

    n8 = tt // SUBLANES

    def steps8(t8, last):
        views = []
        for b in range(PAIR):
            vb = tuple(tile8(ref, b, t8) for ref in (r_ref, w_ref, k_ref, v_ref, kk_ref, kka_ref))
            nxt = kkh_ref.at[b] if last else tile8(kk_ref, b, t8 + 1)
            views.append(vb + (nxt, tile8(y_ref, b, t8)))
        _scan_steps8(views, s_ref, sa_ref, masks)

    def loop_body(t8, carry):
        steps8(t8, False)
        return carry

    lax.fori_loop(0, n8 - 1, loop_body, 0)
    steps8(n8 - 1, True)


_STATE_BLOCK = (PAIR, N_VT, N_KQ, SUBLANES, LANES)


def _scan(r, w, k, v, kk, kka, s0, *, tt):
    bsz, t_len, d = r.shape
    seq = pl.BlockSpec((PAIR, tt, d), lambda i, j: (i, j, 0))
    st = pl.BlockSpec(_STATE_BLOCK, lambda i, j: (i, 0, 0, 0, 0))
    n8 = tt // SUBLANES
    last8 = t_len // SUBLANES - 1
    head = pl.BlockSpec((PAIR, SUBLANES, d), lambda i, j: (i, jnp.minimum((j + 1) * n8, last8), 0))
    return pl.pallas_call(
        functools.partial(_scan_body, tt=tt),
        grid=(bsz // PAIR, t_len // tt),
        in_specs=[seq] * 6 + [head, st],
        out_specs=[seq, st],
        out_shape=[jax.ShapeDtypeStruct((bsz, t_len, d), F32),
                   jax.ShapeDtypeStruct(s0.shape, F32)],
        scratch_shapes=[pltpu.VMEM((PAIR, N_VT, SUBLANES, LANES), F32)],
        compiler_params=_cparams(("parallel", "arbitrary")),
        name="wkv_scan",
    )(r, w, k, v, kk, kka, kk, s0)


def _state_to_tiles(s):
    b = s.shape[0]
    s = s.reshape(b, N_HEADS, N_VT, SUBLANES, N_KQ, K_PER_SLAB)
    return s.transpose(0, 2, 4, 3, 5, 1).reshape(b, N_VT, N_KQ, SUBLANES, LANES)


def _tiles_to_state(s):
    b = s.shape[0]
    s = s.reshape(b, N_VT, N_KQ, SUBLANES, K_PER_SLAB, N_HEADS)
    return s.transpose(0, 5, 1, 3, 2, 4).reshape(b, N_HEADS, HEAD_DIM, HEAD_DIM)


def _post_body(y_ref, v_ref, bc_ref, gl_ref, gtb_ref, gtc_ref, xc_ref, ga_ref, gb_ref, x1_ref,
               bgc_ref, bxc_ref, lng_ref, lnb_ref, cw_ref, gup_ref, wout_ref, x2_o, ut_o, *, seg):
    tm = y_ref.shape[0]
    nseg = tm // seg
    row = lax.broadcasted_iota(jnp.int32, (tm, 1), 0)
    inv_n = 1.0 / HEAD_DIM

    o = y_ref[...]
    mu = _head_sum(o) * inv_n
    c = o - _tile_slabs(mu)
    var = _head_sum(c * c) * inv_n
    o = c * _tile_slabs(lax.rsqrt(var + GN_EPS)) * lng_ref[...] + lnb_ref[...]
    bonus = _tile_slabs(bc_ref[...]) * v_ref[...]
    g = jnp.dot(gl_ref[...], gup_ref[...], preferred_element_type=F32)
    y_a = (o + bonus) * g

    u = gtc_ref[...] * xc_ref[...]
    ub = bgc_ref[...] * bxc_ref[...]
    u1 = pltpu.roll(u, 1, axis=0)
    u2 = pltpu.roll(u, 2, axis=0)
    for s in range(nseg):
        u1 = jnp.where(row == s * seg, ub[2 * s + 1:2 * s + 2, :], u1)
        u2 = jnp.where(row == s * seg, ub[2 * s:2 * s + 1, :], u2)
        u2 = jnp.where(row == s * seg + 1, ub[2 * s + 1:2 * s + 2, :], u2)
    conv = cw_ref[0:1, :] * u2 + cw_ref[1:2, :] * u1 + cw_ref[2:3, :] * u
    y_b = gtb_ref[...] * conv

    merged = _sigmoid(ga_ref[...]) * y_a + _sigmoid(gb_ref[...]) * y_b
    x2_o[...] = x1_ref[...] + jnp.dot(merged.astype(BF16), wout_ref[...], preferred_element_type=F32)

    ut_o[...] = jnp.zeros_like(ut_o)
    for s in range(nseg):
        ut_o[2 * s:2 * s + 2, :] = u[(s + 1) * seg - 2:(s + 1) * seg, :]


def _post(y, v, bc, gl, p, x1, bgc, bxc, lng, lnb, cw, gup, wout, *, tm, seg):
    m, d = y.shape
    rowb = pl.BlockSpec((tm, d), lambda i: (i, 0))
    grp = lambda gi: pl.BlockSpec((tm, d), lambda i, gi=gi: (i, gi))
    vec = pl.BlockSpec((1, d), lambda i: (0, 0))
    bnd = pl.BlockSpec((SUBLANES, d), lambda i: (i, 0))
    return pl.pallas_call(
        functools.partial(_post_body, seg=seg),
        grid=(m // tm,),
        in_specs=[rowb, rowb, pl.BlockSpec((tm, LANES), lambda i: (i, 0)),
                  pl.BlockSpec((tm, LORA_G), lambda i: (i, 0)),
                  grp(3), grp(4), grp(5), grp(6), grp(7), rowb,
                  bnd, bnd, vec, vec,
                  pl.BlockSpec((CONV_W, d), lambda i: (0, 0)),
                  pl.BlockSpec((LORA_G, d), lambda i: (0, 0)),
                  pl.BlockSpec((d, d), lambda i: (0, 0), pipeline_mode=pl.Buffered(1))],
        out_specs=[rowb, bnd],
        out_shape=[jax.ShapeDtypeStruct((m, d), F32),
                   jax.ShapeDtypeStruct((m // tm * SUBLANES, d), F32)],
        compiler_params=_cparams(("parallel",)),
        name="post",
    )(y, v, bc, gl, p, p, p, p, p, x1, bgc, bxc, lng, lnb, cw, gup, wout)


def _hm(x):
    lead = x.shape[:-1]
    return x.reshape(*lead, N_HEADS, HEAD_DIM).swapaxes(-1, -2).reshape(*lead, D_MODEL)


def _hm_groups(x):
    lead = x.shape[:-1]
    n = x.shape[-1] // D_MODEL
    return x.reshape(*lead, n, N_HEADS, HEAD_DIM).swapaxes(-1, -2).reshape(*lead, n * D_MODEL)


def _hm_inv(x):
    lead = x.shape[:-1]
    return x.reshape(*lead, HEAD_DIM, N_HEADS).swapaxes(-1, -2).reshape(*lead, D_MODEL)


def _pad_lora(x):
    z = jnp.zeros(x.shape[:-1] + (LANES - LORA_W,), x.dtype)
    return jnp.concatenate([x[..., :LORA_W], z, x[..., LORA_W:LORA_W + LORA_A], z,
                            x[..., LORA_W + LORA_A:]], axis=-1)


def _pad_rows(x, rows):
    return jnp.concatenate([x, jnp.zeros((rows - x.shape[0],) + x.shape[1:], x.dtype)], axis=0)


def _tile_bounds(rows, n_tiles, n_used):
    w = rows.shape[-1]
    rows = rows.reshape(n_tiles, n_used, w)
    pad = jnp.zeros((n_tiles, SUBLANES - n_used, w), rows.dtype)
    return jnp.concatenate([rows, pad], axis=1).reshape(n_tiles * SUBLANES, w)


def _front(x, shift_prev, wkv_prev, wts, *, tm_prep, seg_prep):
    bsz, t_len, d = x.shape
    m = bsz * t_len
    x1, h, h_last = _ffn(x.reshape(m, d), wts["ffn1_norm"], wts["ffn1_gate"], wts["ffn1_up"], wts["ffn1_down"],
                         wts["mix_norm"], seg=min(t_len, FFN_TM))
    p = _proj(h, wts["w_groups"], tm=min(m, PROJ_TM))

    n_tiles = m // tm_prep
    if shift_prev is None:
        starts = jnp.arange(n_tiles) * tm_prep
        prev = jnp.take(p, jnp.maximum(starts - 1, 0), axis=0)
        prev = jnp.where((starts % t_len == 0)[:, None], 0.0, prev)
        bnd = jnp.concatenate([prev[:, :3 * d], prev[:, N_GROUPS * d:]], axis=1)
        n_used = 1
    else:
        rows = shift_prev.shape[0]
        ws = wts["w_groups"]
        bnd = _proj(_pad_rows(shift_prev, SHIFT_ROWS).astype(BF16), [ws[0], ws[2]], tm=SHIFT_ROWS)[:rows]
        n_used = tm_prep // seg_prep
    bnd = _tile_bounds(bnd, n_tiles, n_used)

    r, w, k, v, kk, kka, gl, bc = _prep(p, bnd, wts["prep_vecs"], wts["prep_ups"], tm=tm_prep, seg=seg_prep)
    sh = (bsz, t_len, d)
    scan_in = tuple(a.reshape(sh) for a in (r, w, k, v, kk, kka)) + (_state_to_tiles(wkv_prev),)

    segs = m // min(t_len, FFN_TM)
    h_last = h_last.reshape(m // FFN_TM, SUBLANES, d)[:, :segs // (m // FFN_TM)].reshape(segs, d)
    shift_new = h_last.reshape(bsz, segs // bsz, d)[:, -1]
    return dict(x1=x1, p=p, v=v, gl=gl, bc=bc, scan_in=scan_in, shift_new=shift_new, shape=sh)


def _merge(fr, y, conv_prev, wts, *, tm_post, seg_post):
    bsz, t_len, d = fr["shape"]
    m = bsz * t_len
    p = fr["p"]
    n_tiles = m // tm_post
    if conv_prev is None:
        starts = jnp.arange(n_tiles) * tm_post
        idx = jnp.maximum(starts[:, None] + jnp.array([-2, -1])[None, :], 0).reshape(-1)
        rows = jnp.take(p, idx, axis=0)
        live = jnp.repeat(starts % t_len != 0, 2)[:, None]
        bgc = jnp.where(live, rows[:, 4 * d:5 * d], 0.0)
        bxc = rows[:, 5 * d:6 * d]
        n_used = 2
    else:
        bgc = _hm(conv_prev).reshape(bsz * (CONV_W - 1), d)
        bxc = jnp.ones_like(bgc)
        n_used = 2 * (tm_post // seg_post)
    bgc = _tile_bounds(bgc, n_tiles, n_used)
    bxc = _tile_bounds(bxc, n_tiles, n_used)

    x2, utail = _post(y.reshape(m, d), fr["v"], fr["bc"], fr["gl"], p, fr["x1"], bgc, bxc,
                      wts["ln_g"], wts["ln_b"], wts["conv_w"], wts["g_up"], wts["w_out"],
                      tm=tm_post, seg=seg_post)
    nseg_post = tm_post // seg_post
    ut = utail.reshape(n_tiles, SUBLANES, d)[:, :2 * nseg_post]
    ut = ut.reshape(n_tiles * nseg_post, 2, d)
    segs_per_seq = t_len // seg_post
    return x2, _hm_inv(ut[segs_per_seq - 1::segs_per_seq])


def kernel(x_prompt, x_sample, state_shift, state_conv, state_wkv, ffn1_norm, ffn1_gate, ffn1_up, ffn1_down, mix_norm, w_in, mu_shift, w0, w_lora_up, a0, a_lora_up, g_lora_up, k_k, k_a, r_k, ln_x_g, ln_x_b, conv_w, w_out, ffn2_norm, ffn2_gate, ffn2_up, ffn2_down, final_norm):
    depth = w_in.shape[0]
    assert depth == 1, "single-layer kernel"
    d = D_MODEL
    n_a = 3 * d + LORA_W + LORA_A + LORA_G
    l = 0

    wi = w_in[l]
    w_groups = [_hm_groups(wi[:, :3 * d].astype(BF16)), _hm_groups(wi[:, n_a:].astype(BF16)),
                _pad_lora(wi[:, 3 * d:n_a].astype(BF16))]

    mu = mu_shift[l]
    row = lambda x: x.reshape(1, -1)
    prep_vecs = (row(_hm(mu[0:d])), row(_hm(mu[d:2 * d])), row(_hm(mu[2 * d:3 * d])),
                 row(_pad_lora(mu[3 * d:])),
                 row(_hm(w0[l])), row(_hm(a0[l])), row(_hm(k_k[l])), row(_hm(k_a[l])),
                 row(_hm(r_k[l].reshape(d))))
    prep_ups = (_pad_rows(_hm(w_lora_up[l]), LANES).astype(BF16),
                _pad_rows(_hm(a_lora_up[l]), LANES).astype(BF16))

    wts = dict(
        ffn1_norm=row(ffn1_norm[l]), ffn1_gate=ffn1_gate[l].astype(BF16), ffn1_up=ffn1_up[l].astype(BF16),
        ffn1_down=ffn1_down[l].astype(BF16), mix_norm=row(mix_norm[l]), w_groups=w_groups,
        prep_vecs=prep_vecs, prep_ups=prep_ups, g_up=_hm(g_lora_up[l]).astype(BF16),
        ln_g=row(_hm(ln_x_g[l])), ln_b=row(_hm(ln_x_b[l])), conv_w=_hm(conv_w[l]),
        w_out=_hm(w_out[l].T).T.astype(BF16),
        ffn2_norm=row(ffn2_norm[l]), ffn2_gate=ffn2_gate[l].astype(BF16), ffn2_up=ffn2_up[l].astype(BF16),
        ffn2_down=ffn2_down[l].astype(BF16), final_norm=row(final_norm),
    )

    b_p, t_p, _ = x_prompt.shape
    b_s, t_s, _ = x_sample.shape
    ffn2 = (wts["ffn2_norm"], wts["ffn2_gate"], wts["ffn2_up"], wts["ffn2_down"], wts["final_norm"])
    zero_state = jnp.zeros((b_p, N_HEADS, HEAD_DIM, HEAD_DIM), state_wkv.dtype)

    fp = _front(x_prompt, None, zero_state, wts, tm_prep=PREP_TM, seg_prep=PREP_TM)
    y_p, s_p = _scan(*fp["scan_in"], tt=min(t_p, SCAN_TT))
    x2_p, cv_p = _merge(fp, y_p, None, wts, tm_post=POST_TM, seg_post=POST_TM)
    (yo_p,) = _ffn(x2_p, *ffn2, seg=None)

    fs = _front(x_sample, state_shift[l], state_wkv[l], wts, tm_prep=PREP_TM, seg_prep=t_s)
    y_s, s_s = _scan(*fs["scan_in"], tt=min(t_s, SCAN_TT))
    x2_s, cv_s = _merge(fs, y_s, state_conv[l], wts, tm_post=POST_TM, seg_post=t_s)
    (yo_s,) = _ffn(x2_s, *ffn2, seg=None)

    return (yo_p.reshape(x_prompt.shape), yo_s.reshape(x_sample.shape),
            fp["shift_new"][None], cv_p[None], _tiles_to_state(s_p)[None],
            fs["shift_new"][None], cv_s[None], _tiles_to_state(s_s)[None])
```

```python
import functools
import math

import jax
import jax.numpy as jnp
from jax import lax
from jax.experimental import pallas as pl
from jax.experimental.pallas import tpu as pltpu

F32 = jnp.float32
BF16 = jnp.bfloat16

HEAD_DIM = 64
N_HEADS = 32
D_MODEL = N_HEADS * HEAD_DIM
CONV_W = 3
LORA_W = 96
LORA_A = 96
LORA_G = 256
RMS_EPS = 1e-6
GN_EPS = 64e-5
EXP_M05 = math.exp(-0.5)

LANES = 128
SUBLANES = 8
N_SLABS = D_MODEL // LANES
K_PER_SLAB = LANES // N_HEADS
LORA_PAD = 512
N_GROUPS = 8
VMEM_LIMIT = 56 * 1024 * 1024
FFN_TM = 512
FFN_TF = 512
PROJ_TM = 2048
PREP_TM = 128
POST_TM = 256
SCAN_TT = 128
SHIFT_ROWS = 32


def _cparams(sem):
    return pltpu.CompilerParams(dimension_semantics=sem, vmem_limit_bytes=VMEM_LIMIT)


def _sigmoid(x):
    return 1.0 / (1.0 + jnp.exp(-x))


def _ffn_body(x_ref, g1_ref, wg_ref, wu_ref, wd_ref, g2_ref, *rest, seg):
    if seg is None:
        ho_ref, xn_ref, acc_ref = rest
    else:
        xo_ref, ho_ref, hl_ref, xn_ref, acc_ref = rest
    j = pl.program_id(1)

    @pl.when(j == 0)
    def _():
        x = x_ref[...]
        ms = jnp.mean(x * x, axis=-1, keepdims=True)
        xn_ref[...] = (x * lax.rsqrt(ms + RMS_EPS) * g1_ref[...]).astype(BF16)
        acc_ref[...] = jnp.zeros_like(acc_ref)

    xn = xn_ref[...]
    gate = jnp.dot(xn, wg_ref[...], preferred_element_type=F32)
    up = jnp.dot(xn, wu_ref[...], preferred_element_type=F32)
    act = (gate * _sigmoid(gate) * up).astype(BF16)
    acc_ref[...] += jnp.dot(act, wd_ref[...], preferred_element_type=F32)

    @pl.when(j == pl.num_programs(1) - 1)
    def _():
        x1 = x_ref[...] + 0.5 * acc_ref[...]
        ms = jnp.mean(x1 * x1, axis=-1, keepdims=True)
        h = x1 * lax.rsqrt(ms + RMS_EPS) * g2_ref[...]
        if seg is None:
            ho_ref[...] = h
        else:
            xo_ref[...] = x1
            ho_ref[...] = h.astype(BF16)
            hl_ref[...] = jnp.zeros_like(hl_ref)
            for s in range(x1.shape[0] // seg):
                hl_ref[s:s + 1, :] = h[(s + 1) * seg - 1:(s + 1) * seg, :]


def _ffn(x, g1, wg, wu, wd, g2, *, seg, tm=FFN_TM, tf=FFN_TF):
    m, d = x.shape
    f = wg.shape[1]
    grid = (m // tm, f // tf)
    row = pl.BlockSpec((tm, d), lambda i, j: (i, 0))
    vec = pl.BlockSpec((1, d), lambda i, j: (0, 0))
    if seg is None:
        out_specs = [row]
        out_shape = [jax.ShapeDtypeStruct((m, d), F32)]
    else:
        assert tm // seg <= SUBLANES
        out_specs = [row, row, pl.BlockSpec((SUBLANES, d), lambda i, j: (i, 0))]
        out_shape = [jax.ShapeDtypeStruct((m, d), F32), jax.ShapeDtypeStruct((m, d), BF16),
                     jax.ShapeDtypeStruct((m // tm * SUBLANES, d), F32)]
    return pl.pallas_call(
        functools.partial(_ffn_body, seg=seg),
        grid=grid,
        in_specs=[row, vec,
                  pl.BlockSpec((d, tf), lambda i, j: (0, j)),
                  pl.BlockSpec((d, tf), lambda i, j: (0, j)),
                  pl.BlockSpec((tf, d), lambda i, j: (j, 0)),
                  vec],
        out_specs=out_specs,
        out_shape=out_shape,
        scratch_shapes=[pltpu.VMEM((tm, d), BF16), pltpu.VMEM((tm, d), F32)],
        compiler_params=_cparams(("parallel", "arbitrary")),
        name="ffn",
    )(x, g1, wg, wu, wd, g2)


def _proj_body(h_ref, *refs, starts):
    w_refs, o_ref = refs[:-1], refs[-1]
    j = pl.program_id(1)
    for g, w_ref in enumerate(w_refs):
        @pl.when((j >= starts[g]) & (j < starts[g + 1]))
        def _(w_ref=w_ref):
            o_ref[...] = jnp.dot(h_ref[...], w_ref[...], preferred_element_type=F32)


def _proj(h, ws, *, tm, tn=512):
    m, d = h.shape
    starts = [0]
    for w in ws:
        starts.append(starts[-1] + w.shape[1] // tn)
    w_specs = [pl.BlockSpec((d, tn), lambda i, j, lo=lo, hi=hi: (0, jnp.clip(j - lo, 0, hi - lo - 1)))
               for lo, hi in zip(starts[:-1], starts[1:])]
    return pl.pallas_call(
        functools.partial(_proj_body, starts=tuple(starts)),
        grid=(m // tm, starts[-1]),
        in_specs=[pl.BlockSpec((tm, d), lambda i, j: (i, 0))] + w_specs,
        out_specs=pl.BlockSpec((tm, tn), lambda i, j: (i, j)),
        out_shape=jax.ShapeDtypeStruct((m, starts[-1] * tn), F32),
        compiler_params=_cparams(("parallel", "arbitrary")),
        name="proj",
    )(h, *ws)


def _slab(x, q):
    return x[:, q * LANES:(q + 1) * LANES]


def _head_sum(x):
    acc = _slab(x, 0)
    for q in range(1, N_SLABS):
        acc = acc + _slab(x, q)
    acc = acc + pltpu.roll(acc, 2 * N_HEADS, axis=1)
    return acc + pltpu.roll(acc, N_HEADS, axis=1)


def _tile_slabs(x):
    return jnp.concatenate([x] * N_SLABS, axis=1)


def _prep_body(pr_ref, pk_ref, pv_ref, pl_ref, br_ref, bk_ref, bv_ref, bl_ref,
               mur_ref, muk_ref, muv_ref, mul_ref, w0_ref, a0_ref, kkp_ref, kap_ref, rkp_ref,
               wup_ref, aup_ref,
               r_o, w_o, k_o, v_o, kk_o, kka_o, gl_o, bc_o, *, seg):
    tm = pr_ref.shape[0]
    nseg = tm // seg
    row = lax.broadcasted_iota(jnp.int32, (tm, 1), 0)

    def shifted(p_ref, b_ref, mu_ref):
        p = p_ref[...]
        prev = pltpu.roll(p, 1, axis=0)
        for s in range(nseg):
            prev = jnp.where(row == s * seg, b_ref[s:s + 1, :], prev)
        return p + mu_ref[...] * (prev - p)

    r = shifted(pr_ref, br_ref, mur_ref)
    k = shifted(pk_ref, bk_ref, muk_ref)
    v = shifted(pv_ref, bv_ref, muv_ref)
    lo = shifted(pl_ref, bl_ref, mul_ref)

    wl = jnp.tanh(lo[:, 0:LANES]).astype(BF16)
    al = lo[:, LANES:2 * LANES].astype(BF16)
    z = w0_ref[...] + jnp.dot(wl, wup_ref[...], preferred_element_type=F32)
    decay = jnp.exp(-EXP_M05 * _sigmoid(z))
    a = _sigmoid(a0_ref[...] + jnp.dot(al, aup_ref[...], preferred_element_type=F32))

    kk = k * kkp_ref[...]
    norm = jnp.maximum(jnp.sqrt(_head_sum(kk * kk)), 1e-12)
    kk = kk * _tile_slabs(1.0 / norm)
    k_mod = k * (1.0 + (a - 1.0) * kap_ref[...])

    pos = row % SUBLANES
    q = decay
    for sh in (1, 2, 4):
        q = q * jnp.where(pos >= sh, pltpu.roll(q, sh, axis=0), 1.0)
    inv_q = 1.0 / q

    r_o[...] = r * q
    w_o[...] = q
    k_o[...] = k_mod * inv_q
    v_o[...] = v
    kk_o[...] = kk
    kka_o[...] = kk * a * inv_q
    gl_o[...] = _sigmoid(lo[:, 2 * LANES:LORA_PAD]).astype(BF16)
    bc_o[...] = _head_sum(r * k_mod * rkp_ref[...])


def _prep(p, bnd, vecs, ups, *, tm, seg):
    m = p.shape[0]
    d = D_MODEL
    lora_blk = N_GROUPS * d // LORA_PAD
    grp = lambda g: pl.BlockSpec((tm, d), lambda i, g=g: (i, g))
    bgrp = lambda g: pl.BlockSpec((SUBLANES, d), lambda i, g=g: (i, g))
    vec = pl.BlockSpec((1, d), lambda i: (0, 0))
    row_out = pl.BlockSpec((tm, d), lambda i: (i, 0))
    mur, muk, muv, mul, w0, a0, kkp, kap, rkp = vecs
    wup, aup = ups
    return pl.pallas_call(
        functools.partial(_prep_body, seg=seg),
        grid=(m // tm,),
        in_specs=[grp(0), grp(1), grp(2),
                  pl.BlockSpec((tm, LORA_PAD), lambda i: (i, lora_blk)),
                  bgrp(0), bgrp(1), bgrp(2),
                  pl.BlockSpec((SUBLANES, LORA_PAD), lambda i: (i, 3 * d // LORA_PAD)),
                  vec, vec, vec, pl.BlockSpec((1, LORA_PAD), lambda i: (0, 0)),
                  vec, vec, vec, vec, vec,
                  pl.BlockSpec((LANES, d), lambda i: (0, 0)),
                  pl.BlockSpec((LANES, d), lambda i: (0, 0))],
        out_specs=[row_out] * 6 + [pl.BlockSpec((tm, LORA_G), lambda i: (i, 0)),
                                   pl.BlockSpec((tm, LANES), lambda i: (i, 0))],
        out_shape=[jax.ShapeDtypeStruct((m, d), F32)] * 6 + [jax.ShapeDtypeStruct((m, LORA_G), BF16),
                                                             jax.ShapeDtypeStruct((m, LANES), F32)],
        compiler_params=_cparams(("parallel",)),
        name="prep",
    )(p, p, p, p, bnd, bnd, bnd, bnd, mur, muk, muv, mul, w0, a0, kkp, kap, rkp, wup, aup)


N_VT = HEAD_DIM // SUBLANES
N_KQ = N_SLABS
PAIR = 2


def _group_sum(x):
    x = x + pltpu.roll(x, 2 * N_HEADS, axis=1)
    return x + pltpu.roll(x, N_HEADS, axis=1)


def _scan_masks():
    sub = lax.broadcasted_iota(jnp.int32, (SUBLANES, LANES), 0)
    grp = lax.broadcasted_iota(jnp.int32, (SUBLANES, LANES), 1) // N_HEADS
    rot = (grp - sub % K_PER_SLAB) % K_PER_SLAB
    return rot, sub >= K_PER_SLAB, grp == sub % K_PER_SLAB


def _bcast_row(view, s, q):
    return jnp.broadcast_to(view[pl.ds(s, 1), pl.ds(q * LANES, LANES)], (SUBLANES, LANES))


def _scan_init(s0_ref, s_ref, sa_ref, kk_views):
    s_ref[...] = s0_ref[...]
    for b in range(PAIR):
        for j in range(N_VT):
            acc = s0_ref[b, j, 0] * _bcast_row(kk_views[b], 0, 0)
            for kq in range(1, N_KQ):
                acc = acc + s0_ref[b, j, kq] * _bcast_row(kk_views[b], 0, kq)
            sa_ref[b, j] = _group_sum(acc)


def _scan_step(b, views, s, js, s_ref, sa_ref, masks):
    rot, upper, diag = masks
    r_v, w_v, k_v, v_v, kk_v, kka_v, kkn_v, y_v = views
    vcols = {}
    for j in js:
        base = jnp.where(upper, _bcast_row(v_v, s, 2 * j + 1), _bcast_row(v_v, s, 2 * j))
        col = base
        for d in range(1, K_PER_SLAB):
            col = jnp.where(rot == d, pltpu.roll(base, d * N_HEADS, axis=1), col)
        vcols[j] = col
    sas = {j: sa_ref[b, j] for j in js}
    acc_y = {}
    acc_s = {}
    for kq in range(N_KQ):
        q_row = _bcast_row(w_v, s, kq)
        kka_row = _bcast_row(kka_v, s, kq)
        k_row = _bcast_row(k_v, s, kq)
        r_row = _bcast_row(r_v, s, kq)
        kkn_row = _bcast_row(kk_v, s + 1, kq) if s + 1 < SUBLANES else _bcast_row(kkn_v, 0, kq)
        qkn_row = q_row * kkn_row
        for j in js:
            s_new = s_ref[b, j, kq] - sas[j] * kka_row + vcols[j] * k_row
            s_ref[b, j, kq] = s_new * q_row if s == SUBLANES - 1 else s_new
            ty = s_new * r_row
            ts = s_new * qkn_row
            acc_y[j] = ty if kq == 0 else acc_y[j] + ty
            acc_s[j] = ts if kq == 0 else acc_s[j] + ts
    for j in js:
        sa_ref[b, j] = _group_sum(acc_s[j])
        yv = jnp.where(diag, _group_sum(acc_y[j]), 0.0)
        yv = yv + pltpu.roll(yv, 1, axis=0)
        yv = yv + pltpu.roll(yv, 2, axis=0)
        y_v[pl.ds(s, 1), pl.ds(2 * j * LANES, LANES)] = yv[K_PER_SLAB - 1:K_PER_SLAB, :]
        y_v[pl.ds(s, 1), pl.ds((2 * j + 1) * LANES, LANES)] = yv[SUBLANES - 1:SUBLANES, :]


def _scan_steps8(views, s_ref, sa_ref, masks):
    for s in range(SUBLANES):
        for b in range(PAIR):
            _scan_step(b, views[b], s, tuple(range(N_VT)), s_ref, sa_ref, masks)


def _scan_body(r_ref, w_ref, k_ref, v_ref, kk_ref, kka_ref, kkh_ref, s0_ref, y_ref, s_ref, sa_ref, *, tt):
    masks = _scan_masks()

    def tile8(ref, b, t8):
        start = t8 * SUBLANES
        if not isinstance(t8, int):
            start = pl.multiple_of(start, SUBLANES)
        return ref.at[b, pl.ds(start, SUBLANES), :]

    @pl.when(pl.program_id(1) == 0)
    def _():
        _scan_init(s0_ref, s_ref, sa_ref, [tile8(kk_ref, b, 0) for b in range(PAIR)])

    n8 = tt // SUBLANES

    def steps8(t8, last):
        views = []
        for b in range(PAIR):
            vb = tuple(tile8(ref, b, t8) for ref in (r_ref, w_ref, k_ref, v_ref, kk_ref, kka_ref))
            nxt = kkh_ref.at[b] if last else tile8(kk_ref, b, t8 + 1)
            views.append(vb + (nxt, tile8(y_ref, b, t8)))
        _scan_steps8(views, s_ref, sa_ref, masks)

    def loop_body(t8, carry):
        steps8(t8, False)
        return carry

    lax.fori_loop(0, n8 - 1, loop_body, 0)
    steps8(n8 - 1, True)


_STATE_BLOCK = (PAIR, N_VT, N_KQ, SUBLANES, LANES)


def _scan(r, w, k, v, kk, kka, s0, *, tt):
    bsz, t_len, d = r.shape
    seq = pl.BlockSpec((PAIR, tt, d), lambda i, j: (i, j, 0))
    st = pl.BlockSpec(_STATE_BLOCK, lambda i, j: (i, 0, 0, 0, 0))
    n8 = tt // SUBLANES
    last8 = t_len // SUBLANES - 1
    head = pl.BlockSpec((PAIR, SUBLANES, d), lambda i, j: (i, jnp.minimum((j + 1) * n8, last8), 0))
    return pl.pallas_call(
        functools.partial(_scan_body, tt=tt),
        grid=(bsz // PAIR, t_len // tt),
        in_specs=[seq] * 6 + [head, st],
        out_specs=[seq, st],
        out_shape=[jax.ShapeDtypeStruct((bsz, t_len, d), F32),
                   jax.ShapeDtypeStruct(s0.shape, F32)],
        scratch_shapes=[pltpu.VMEM((PAIR, N_VT, SUBLANES, LANES), F32)],
        compiler_params=_cparams(("parallel", "arbitrary")),
        name="wkv_scan",
    )(r, w, k, v, kk, kka, kk, s0)


def _state_to_tiles(s):
    b = s.shape[0]
    s = s.reshape(b, N_HEADS, N_VT, SUBLANES, N_KQ, K_PER_SLAB)
    return s.transpose(0, 2, 4, 3, 5, 1).reshape(b, N_VT, N_KQ, SUBLANES, LANES)


def _tiles_to_state(s):
    b = s.shape[0]
    s = s.reshape(b, N_VT, N_KQ, SUBLANES, K_PER_SLAB, N_HEADS)
    return s.transpose(0, 5, 1, 3, 2, 4).reshape(b, N_HEADS, HEAD_DIM, HEAD_DIM)


def _post_body(y_ref, v_ref, bc_ref, gl_ref, gtb_ref, gtc_ref, xc_ref, ga_ref, gb_ref, x1_ref,
               bgc_ref, bxc_ref, lng_ref, lnb_ref, cw_ref, gup_ref, wout_ref, x2_o, ut_o, *, seg):
    tm = y_ref.shape[0]
    nseg = tm // seg
    row = lax.broadcasted_iota(jnp.int32, (tm, 1), 0)
    inv_n = 1.0 / HEAD_DIM

    o = y_ref[...]
    mu = _head_sum(o) * inv_n
    c = o - _tile_slabs(mu)
    var = _head_sum(c * c) * inv_n
    o = c * _tile_slabs(lax.rsqrt(var + GN_EPS)) * lng_ref[...] + lnb_ref[...]
    bonus = _tile_slabs(bc_ref[...]) * v_ref[...]
    g = jnp.dot(gl_ref[...], gup_ref[...], preferred_element_type=F32)
    y_a = (o + bonus) * g

    u = gtc_ref[...] * xc_ref[...]
    ub = bgc_ref[...] * bxc_ref[...]
    u1 = pltpu.roll(u, 1, axis=0)
    u2 = pltpu.roll(u, 2, axis=0)
    for s in range(nseg):
        u1 = jnp.where(row == s * seg, ub[2 * s + 1:2 * s + 2, :], u1)
        u2 = jnp.where(row == s * seg, ub[2 * s:2 * s + 1, :], u2)
        u2 = jnp.where(row == s * seg + 1, ub[2 * s + 1:2 * s + 2, :], u2)
    conv = cw_ref[0:1, :] * u2 + cw_ref[1:2, :] * u1 + cw_ref[2:3, :] * u
    y_b = gtb_ref[...] * conv

    merged = _sigmoid(ga_ref[...]) * y_a + _sigmoid(gb_ref[...]) * y_b
    x2_o[...] = x1_ref[...] + jnp.dot(merged.astype(BF16), wout_ref[...], preferred_element_type=F32)

    ut_o[...] = jnp.zeros_like(ut_o)
    for s in range(nseg):
        ut_o[2 * s:2 * s + 2, :] = u[(s + 1) * seg - 2:(s + 1) * seg, :]


def _post(y, v, bc, gl, p, x1, bgc, bxc, lng, lnb, cw, gup, wout, *, tm, seg):
    m, d = y.shape
    rowb = pl.BlockSpec((tm, d), lambda i: (i, 0))
    grp = lambda gi: pl.BlockSpec((tm, d), lambda i, gi=gi: (i, gi))
    vec = pl.BlockSpec((1, d), lambda i: (0, 0))
    bnd = pl.BlockSpec((SUBLANES, d), lambda i: (i, 0))
    return pl.pallas_call(
        functools.partial(_post_body, seg=seg),
        grid=(m // tm,),
        in_specs=[rowb, rowb, pl.BlockSpec((tm, LANES), lambda i: (i, 0)),
                  pl.BlockSpec((tm, LORA_G), lambda i: (i, 0)),
                  grp(3), grp(4), grp(5), grp(6), grp(7), rowb,
                  bnd, bnd, vec, vec,
                  pl.BlockSpec((CONV_W, d), lambda i: (0, 0)),
                  pl.BlockSpec((LORA_G, d), lambda i: (0, 0)),
                  pl.BlockSpec((d, d), lambda i: (0, 0), pipeline_mode=pl.Buffered(1))],
        out_specs=[rowb, bnd],
        out_shape=[jax.ShapeDtypeStruct((m, d), F32),
                   jax.ShapeDtypeStruct((m // tm * SUBLANES, d), F32)],
        compiler_params=_cparams(("parallel",)),
        name="post",
    )(y, v, bc, gl, p, p, p, p, p, x1, bgc, bxc, lng, lnb, cw, gup, wout)


def _hm(x):
    lead = x.shape[:-1]
    return x.reshape(*lead, N_HEADS, HEAD_DIM).swapaxes(-1, -2).reshape(*lead, D_MODEL)


def _hm_groups(x):
    lead = x.shape[:-1]
    n = x.shape[-1] // D_MODEL
    return x.reshape(*lead, n, N_HEADS, HEAD_DIM).swapaxes(-1, -2).reshape(*lead, n * D_MODEL)


def _hm_inv(x):
    lead = x.shape[:-1]
    return x.reshape(*lead, HEAD_DIM, N_HEADS).swapaxes(-1, -2).reshape(*lead, D_MODEL)


def _pad_lora(x):
    z = jnp.zeros(x.shape[:-1] + (LANES - LORA_W,), x.dtype)
    return jnp.concatenate([x[..., :LORA_W], z, x[..., LORA_W:LORA_W + LORA_A], z,
                            x[..., LORA_W + LORA_A:]], axis=-1)


def _pad_rows(x, rows):
    return jnp.concatenate([x, jnp.zeros((rows - x.shape[0],) + x.shape[1:], x.dtype)], axis=0)


def _tile_bounds(rows, n_tiles, n_used):
    w = rows.shape[-1]
    rows = rows.reshape(n_tiles, n_used, w)
    pad = jnp.zeros((n_tiles, SUBLANES - n_used, w), rows.dtype)
    return jnp.concatenate([rows, pad], axis=1).reshape(n_tiles * SUBLANES, w)


def _front(x, shift_prev, wkv_prev, wts, *, tm_prep, seg_prep):
    bsz, t_len, d = x.shape
    m = bsz * t_len
    x1, h, h_last = _ffn(x.reshape(m, d), wts["ffn1_norm"], wts["ffn1_gate"], wts["ffn1_up"], wts["ffn1_down"],
                         wts["mix_norm"], seg=min(t_len, FFN_TM))
    p = _proj(h, wts["w_groups"], tm=min(m, PROJ_TM))

    n_tiles = m // tm_prep
    if shift_prev is None:
        starts = jnp.arange(n_tiles) * tm_prep
        prev = jnp.take(p, jnp.maximum(starts - 1, 0), axis=0)
        prev = jnp.where((starts % t_len == 0)[:, None], 0.0, prev)
        bnd = jnp.concatenate([prev[:, :3 * d], prev[:, N_GROUPS * d:]], axis=1)
        n_used = 1
    else:
        rows = shift_prev.shape[0]
        ws = wts["w_groups"]
        bnd = _proj(_pad_rows(shift_prev, SHIFT_ROWS).astype(BF16), [ws[0], ws[2]], tm=SHIFT_ROWS)[:rows]
        n_used = tm_prep // seg_prep
    bnd = _tile_bounds(bnd, n_tiles, n_used)

    r, w, k, v, kk, kka, gl, bc = _prep(p, bnd, wts["prep_vecs"], wts["prep_ups"], tm=tm_prep, seg=seg_prep)
    sh = (bsz, t_len, d)
    scan_in = tuple(a.reshape(sh) for a in (r, w, k, v, kk, kka)) + (_state_to_tiles(wkv_prev),)

    segs = m // min(t_len, FFN_TM)
    h_last = h_last.reshape(m // FFN_TM, SUBLANES, d)[:, :segs // (m // FFN_TM)].reshape(segs, d)
    shift_new = h_last.reshape(bsz, segs // bsz, d)[:, -1]
    return dict(x1=x1, p=p, v=v, gl=gl, bc=bc, scan_in=scan_in, shift_new=shift_new, shape=sh)


def _merge(fr, y, conv_prev, wts, *, tm_post, seg_post):
    bsz, t_len, d = fr["shape"]
    m = bsz * t_len
    p = fr["p"]
    n_tiles = m // tm_post
    if conv_prev is None:
        starts = jnp.arange(n_tiles) * tm_post
        idx = jnp.maximum(starts[:, None] + jnp.array([-2, -1])[None, :], 0).reshape(-1)
        rows = jnp.take(p, idx, axis=0)
        live = jnp.repeat(starts % t_len != 0, 2)[:, None]
        bgc = jnp.where(live, rows[:, 4 * d:5 * d], 0.0)
        bxc = rows[:, 5 * d:6 * d]
        n_used = 2
    else:
        bgc = _hm(conv_prev).reshape(bsz * (CONV_W - 1), d)
        bxc = jnp.ones_like(bgc)
        n_used = 2 * (tm_post // seg_post)
    bgc = _tile_bounds(bgc, n_tiles, n_used)
    bxc = _tile_bounds(bxc, n_tiles, n_used)

    x2, utail = _post(y.reshape(m, d), fr["v"], fr["bc"], fr["gl"], p, fr["x1"], bgc, bxc,
                      wts["ln_g"], wts["ln_b"], wts["conv_w"], wts["g_up"], wts["w_out"],
                      tm=tm_post, seg=seg_post)
    nseg_post = tm_post // seg_post
    ut = utail.reshape(n_tiles, SUBLANES, d)[:, :2 * nseg_post]
    ut = ut.reshape(n_tiles * nseg_post, 2, d)
    segs_per_seq = t_len // seg_post
    return x2, _hm_inv(ut[segs_per_seq - 1::segs_per_seq])


def kernel(x_prompt, x_sample, state_shift, state_conv, state_wkv, ffn1_norm, ffn1_gate, ffn1_up, ffn1_down, mix_norm, w_in, mu_shift, w0, w_lora_up, a0, a_lora_up, g_lora_up, k_k, k_a, r_k, ln_x_g, ln_x_b, conv_w, w_out, ffn2_norm, ffn2_gate, ffn2_up, ffn2_down, final_norm):
    depth = w_in.shape[0]
    assert depth == 1, "single-layer kernel"
    d = D_MODEL
    n_a = 3 * d + LORA_W + LORA_A + LORA_G
    l = 0

    wi = w_in[l]
    w_groups = [_hm_groups(wi[:, :3 * d].astype(BF16)), _hm_groups(wi[:, n_a:].astype(BF16)),
                _pad_lora(wi[:, 3 * d:n_a].astype(BF16))]

    mu = mu_shift[l]
    row = lambda x: x.reshape(1, -1)
    prep_vecs = (row(_hm(mu[0:d])), row(_hm(mu[d:2 * d])), row(_hm(mu[2 * d:3 * d])),
                 row(_pad_lora(mu[3 * d:])),
                 row(_hm(w0[l])), row(_hm(a0[l])), row(_hm(k_k[l])), row(_hm(k_a[l])),
                 row(_hm(r_k[l].reshape(d))))
    prep_ups = (_pad_rows(_hm(w_lora_up[l]), LANES).astype(BF16),
                _pad_rows(_hm(a_lora_up[l]), LANES).astype(BF16))

    wts = dict(
        ffn1_norm=row(ffn1_norm[l]), ffn1_gate=ffn1_gate[l].astype(BF16), ffn1_up=ffn1_up[l].astype(BF16),
        ffn1_down=ffn1_down[l].astype(BF16), mix_norm=row(mix_norm[l]), w_groups=w_groups,
        prep_vecs=prep_vecs, prep_ups=prep_ups, g_up=_hm(g_lora_up[l]).astype(BF16),
        ln_g=row(_hm(ln_x_g[l])), ln_b=row(_hm(ln_x_b[l])), conv_w=_hm(conv_w[l]),
        w_out=_hm(w_out[l].T).T.astype(BF16),
        ffn2_norm=row(ffn2_norm[l]), ffn2_gate=ffn2_gate[l].astype(BF16), ffn2_up=ffn2_up[l].astype(BF16),
        ffn2_down=ffn2_down[l].astype(BF16), final_norm=row(final_norm),
    )

    b_p, t_p, _ = x_prompt.shape
    b_s, t_s, _ = x_sample.shape
    ffn2 = (wts["ffn2_norm"], wts["ffn2_gate"], wts["ffn2_up"], wts["ffn2_down"], wts["final_norm"])
    zero_state = jnp.zeros((b_p, N_HEADS, HEAD_DIM, HEAD_DIM), state_wkv.dtype)

    fp = _front(x_prompt, None, zero_state, wts, tm_prep=PREP_TM, seg_prep=PREP_TM)
    y_p, s_p = _scan(*fp["scan_in"], tt=min(t_p, SCAN_TT))
    x2_p, cv_p = _merge(fp, y_p, None, wts, tm_post=POST_TM, seg_post=POST_TM)
    (yo_p,) = _ffn(x2_p, *ffn2, seg=None)

    fs = _front(x_sample, state_shift[l], state_wkv[l], wts, tm_prep=PREP_TM, seg_prep=t_s)
    y_s, s_s = _scan(*fs["scan_in"], tt=min(t_s, SCAN_TT))
    x2_s, cv_s = _merge(fs, y_s, state_conv[l], wts, tm_post=POST_TM, seg_post=t_s)
    (yo_s,) = _ffn(x2_s, *ffn2, seg=None)

    return (yo_p.reshape(x_prompt.shape), yo_s.reshape(x_sample.shape),
            fp["shift_new"][None], cv_p[None], _tiles_to_state(s_p)[None],
            fs["shift_new"][None], cv_s[None], _tiles_to_state(s_s)[None])
```

```python
import functools
import math

import jax
import jax.numpy as jnp
from jax import lax
from jax.experimental import pallas as pl
from jax.experimental.pallas import tpu as pltpu

F32 = jnp.float32
BF16 = jnp.bfloat16

HEAD_DIM = 64
N_HEADS = 32
D_MODEL = N_HEADS * HEAD_DIM
CONV_W = 3
LORA_W = 96
LORA_A = 96
LORA_G = 256
RMS_EPS = 1e-6
GN_EPS = 64e-5
EXP_M05 = math.exp(-0.5)

LANES = 128
SUBLANES = 8
N_SLABS = D_MODEL // LANES
K_PER_SLAB = LANES // N_HEADS
LORA_PAD = 512
N_GROUPS = 8
VMEM_LIMIT = 56 * 1024 * 1024
FFN_TM = 512
FFN_TF = 512
PROJ_TM = 2048
PREP_TM = 128
POST_TM = 256
SCAN_TT = 128
SHIFT_ROWS = 32


def _cparams(sem):
    return pltpu.CompilerParams(dimension_semantics=sem, vmem_limit_bytes=VMEM_LIMIT)


def _sigmoid(x):
    return 1.0 / (1.0 + jnp.exp(-x))


def _ffn_body(x_ref, g1_ref, wg_ref, wu_ref, wd_ref, g2_ref, *rest, seg):
    if seg is None:
        ho_ref, xn_ref, acc_ref = rest
    else:
        xo_ref, ho_ref, hl_ref, xn_ref, acc_ref = rest
    j = pl.program_id(1)

    @pl.when(j == 0)
    def _():
        x = x_ref[...]
        ms = jnp.mean(x * x, axis=-1, keepdims=True)
        xn_ref[...] = (x * lax.rsqrt(ms + RMS_EPS) * g1_ref[...]).astype(BF16)
        acc_ref[...] = jnp.zeros_like(acc_ref)

    xn = xn_ref[...]
    half = wg_ref.shape[1] // 2
    acts = []
    for c in range(2):
        cols = pl.ds(c * half, half)
        gate = jnp.dot(xn, wg_ref[:, cols], preferred_element_type=F32)
        up = jnp.dot(xn, wu_ref[:, cols], preferred_element_type=F32)
        acts.append((gate * _sigmoid(gate) * up).astype(BF16))
    act = jnp.concatenate(acts, axis=1)
    acc_ref[...] += jnp.dot(act, wd_ref[...], preferred_element_type=F32)

    @pl.when(j == pl.num_programs(1) - 1)
    def _():
        x1 = x_ref[...] + 0.5 * acc_ref[...]
        ms = jnp.mean(x1 * x1, axis=-1, keepdims=True)
        h = x1 * lax.rsqrt(ms + RMS_EPS) * g2_ref[...]
        if seg is None:
            ho_ref[...] = h
        else:
            xo_ref[...] = x1
            ho_ref[...] = h.astype(BF16)
            hl_ref[...] = jnp.zeros_like(hl_ref)
            for s in range(x1.shape[0] // seg):
                hl_ref[s:s + 1, :] = h[(s + 1) * seg - 1:(s + 1) * seg, :]


def _ffn(x, g1, wg, wu, wd, g2, *, seg, tm=FFN_TM, tf=FFN_TF):
    m, d = x.shape
    f = wg.shape[1]
    grid = (m // tm, f // tf)
    row = pl.BlockSpec((tm, d), lambda i, j: (i, 0))
    vec = pl.BlockSpec((1, d), lambda i, j: (0, 0))
    if seg is None:
        out_specs = [row]
        out_shape = [jax.ShapeDtypeStruct((m, d), F32)]
    else:
        assert tm // seg <= SUBLANES
        out_specs = [row, row, pl.BlockSpec((SUBLANES, d), lambda i, j: (i, 0))]
        out_shape = [jax.ShapeDtypeStruct((m, d), F32), jax.ShapeDtypeStruct((m, d), BF16),
                     jax.ShapeDtypeStruct((m // tm * SUBLANES, d), F32)]
    return pl.pallas_call(
        functools.partial(_ffn_body, seg=seg),
        grid=grid,
        in_specs=[row, vec,
                  pl.BlockSpec((d, tf), lambda i, j: (0, j)),
                  pl.BlockSpec((d, tf), lambda i, j: (0, j)),
                  pl.BlockSpec((tf, d), lambda i, j: (j, 0)),
                  vec],
        out_specs=out_specs,
        out_shape=out_shape,
        scratch_shapes=[pltpu.VMEM((tm, d), BF16), pltpu.VMEM((tm, d), F32)],
        compiler_params=_cparams(("parallel", "arbitrary")),
        name="ffn",
    )(x, g1, wg, wu, wd, g2)


def _proj_body(h_ref, *refs, starts):
    w_refs, o_ref = refs[:-1], refs[-1]
    j = pl.program_id(1)
    for g, w_ref in enumerate(w_refs):
        @pl.when((j >= starts[g]) & (j < starts[g + 1]))
        def _(w_ref=w_ref):
            o_ref[...] = jnp.dot(h_ref[...], w_ref[...], preferred_element_type=F32)


def _proj(h, ws, *, tm, tn=512):
    m, d = h.shape
    starts = [0]
    for w in ws:
        starts.append(starts[-1] + w.shape[1] // tn)
    w_specs = [pl.BlockSpec((d, tn), lambda i, j, lo=lo, hi=hi: (0, jnp.clip(j - lo, 0, hi - lo - 1)))
               for lo, hi in zip(starts[:-1], starts[1:])]
    return pl.pallas_call(
        functools.partial(_proj_body, starts=tuple(starts)),
        grid=(m // tm, starts[-1]),
        in_specs=[pl.BlockSpec((tm, d), lambda i, j: (i, 0))] + w_specs,
        out_specs=pl.BlockSpec((tm, tn), lambda i, j: (i, j)),
        out_shape=jax.ShapeDtypeStruct((m, starts[-1] * tn), F32),
        compiler_params=_cparams(("parallel", "arbitrary")),
        name="proj",
    )(h, *ws)


def _slab(x, q):
    return x[:, q * LANES:(q + 1) * LANES]


def _head_sum(x):
    acc = _slab(x, 0)
    for q in range(1, N_SLABS):
        acc = acc + _slab(x, q)
    acc = acc + pltpu.roll(acc, 2 * N_HEADS, axis=1)
    return acc + pltpu.roll(acc, N_HEADS, axis=1)


def _tile_slabs(x):
    return jnp.concatenate([x] * N_SLABS, axis=1)


def _prep_body(pr_ref, pk_ref, pv_ref, pl_ref, br_ref, bk_ref, bv_ref, bl_ref,
               mur_ref, muk_ref, muv_ref, mul_ref, w0_ref, a0_ref, kkp_ref, kap_ref, rkp_ref,
               wup_ref, aup_ref,
               r_o, w_o, k_o, v_o, kk_o, kka_o, gl_o, bc_o, *, seg):
    tm = pr_ref.shape[0]
    nseg = tm // seg
    row = lax.broadcasted_iota(jnp.int32, (tm, 1), 0)

    def shifted(p_ref, b_ref, mu_ref):
        p = p_ref[...]
        prev = pltpu.roll(p, 1, axis=0)
        for s in range(nseg):
            prev = jnp.where(row == s * seg, b_ref[s:s + 1, :], prev)
        return p + mu_ref[...] * (prev - p)

    r = shifted(pr_ref, br_ref, mur_ref)
    k = shifted(pk_ref, bk_ref, muk_ref)
    v = shifted(pv_ref, bv_ref, muv_ref)
    lo = shifted(pl_ref, bl_ref, mul_ref)

    wl = jnp.tanh(lo[:, 0:LANES]).astype(BF16)
    al = lo[:, LANES:2 * LANES].astype(BF16)
    z = w0_ref[...] + jnp.dot(wl, wup_ref[...], preferred_element_type=F32)
    decay = jnp.exp(-EXP_M05 * _sigmoid(z))
    a = _sigmoid(a0_ref[...] + jnp.dot(al, aup_ref[...], preferred_element_type=F32))

    kk = k * kkp_ref[...]
    norm = jnp.maximum(jnp.sqrt(_head_sum(kk * kk)), 1e-12)
    kk = kk * _tile_slabs(1.0 / norm)
    k_mod = k * (1.0 + (a - 1.0) * kap_ref[...])

    pos = row % SUBLANES
    q = decay
    for sh in (1, 2, 4):
        q = q * jnp.where(pos >= sh, pltpu.roll(q, sh, axis=0), 1.0)
    inv_q = 1.0 / q

    r_o[...] = r * q
    w_o[...] = q
    k_o[...] = k_mod * inv_q
    v_o[...] = v
    kk_o[...] = kk
    kka_o[...] = kk * a * inv_q
    gl_o[...] = _sigmoid(lo[:, 2 * LANES:LORA_PAD]).astype(BF16)
    bc_o[...] = _head_sum(r * k_mod * rkp_ref[...])


def _prep(p, bnd, vecs, ups, *, tm, seg):
    m = p.shape[0]
    d = D_MODEL
    lora_blk = N_GROUPS * d // LORA_PAD
    grp = lambda g: pl.BlockSpec((tm, d), lambda i, g=g: (i, g))
    bgrp = lambda g: pl.BlockSpec((SUBLANES, d), lambda i, g=g: (i, g))
    vec = pl.BlockSpec((1, d), lambda i: (0, 0))
    row_out = pl.BlockSpec((tm, d), lambda i: (i, 0))
    mur, muk, muv, mul, w0, a0, kkp, kap, rkp = vecs
    wup, aup = ups
    return pl.pallas_call(
        functools.partial(_prep_body, seg=seg),
        grid=(m // tm,),
        in_specs=[grp(0), grp(1), grp(2),
                  pl.BlockSpec((tm, LORA_PAD), lambda i: (i, lora_blk)),
                  bgrp(0), bgrp(1), bgrp(2),
                  pl.BlockSpec((SUBLANES, LORA_PAD), lambda i: (i, 3 * d // LORA_PAD)),
                  vec, vec, vec, pl.BlockSpec((1, LORA_PAD), lambda i: (0, 0)),
                  vec, vec, vec, vec, vec,
                  pl.BlockSpec((LANES, d), lambda i: (0, 0)),
                  pl.BlockSpec((LANES, d), lambda i: (0, 0))],
        out_specs=[row_out] * 6 + [pl.BlockSpec((tm, LORA_G), lambda i: (i, 0)),
                                   pl.BlockSpec((tm, LANES), lambda i: (i, 0))],
        out_shape=[jax.ShapeDtypeStruct((m, d), F32)] * 6 + [jax.ShapeDtypeStruct((m, LORA_G), BF16),
                                                             jax.ShapeDtypeStruct((m, LANES), F32)],
        compiler_params=_cparams(("parallel",)),
        name="prep",
    )(p, p, p, p, bnd, bnd, bnd, bnd, mur, muk, muv, mul, w0, a0, kkp, kap, rkp, wup, aup)


N_VT = HEAD_DIM // SUBLANES
N_KQ = N_SLABS
PAIR = 2


def _group_sum(x):
    x = x + pltpu.roll(x, 2 * N_HEADS, axis=1)
    return x + pltpu.roll(x, N_HEADS, axis=1)


def _scan_masks():
    sub = lax.broadcasted_iota(jnp.int32, (SUBLANES, LANES), 0)
    grp = lax.broadcasted_iota(jnp.int32, (SUBLANES, LANES), 1) // N_HEADS
    rot = (grp - sub % K_PER_SLAB) % K_PER_SLAB
    return rot, sub >= K_PER_SLAB, grp == sub % K_PER_SLAB


def _bcast_row(view, s, q):
    return jnp.broadcast_to(view[pl.ds(s, 1), pl.ds(q * LANES, LANES)], (SUBLANES, LANES))


def _scan_init(s0_ref, s_ref, sa_ref, kk_views):
    s_ref[...] = s0_ref[...]
    for b in range(PAIR):
        for j in range(N_VT):
            acc = s0_ref[b, j, 0] * _bcast_row(kk_views[b], 0, 0)
            for kq in range(1, N_KQ):
                acc = acc + s0_ref[b, j, kq] * _bcast_row(kk_views[b], 0, kq)
            sa_ref[b, j] = _group_sum(acc)


def _scan_step(b, views, s, js, s_ref, sa_ref, masks):
    rot, upper, diag = masks
    r_v, w_v, k_v, v_v, kk_v, kka_v, kkn_v, y_v = views
    vcols = {}
    for j in js:
        base = jnp.where(upper, _bcast_row(v_v, s, 2 * j + 1), _bcast_row(v_v, s, 2 * j))
        col = base
        for d in range(1, K_PER_SLAB):
            col = jnp.where(rot == d, pltpu.roll(base, d * N_HEADS, axis=1), col)
        vcols[j] = col
    sas = {j: sa_ref[b, j] for j in js}
    acc_y = {}
    acc_s = {}
    for kq in range(N_KQ):
        q_row = _bcast_row(w_v, s, kq)
        kka_row = _bcast_row(kka_v, s, kq)
        k_row = _bcast_row(k_v, s, kq)
        r_row = _bcast_row(r_v, s, kq)
        kkn_row = _bcast_row(kk_v, s + 1, kq) if s + 1 < SUBLANES else _bcast_row(kkn_v, 0, kq)
        qkn_row = q_row * kkn_row
        for j in js:
            s_new = s_ref[b, j, kq] - sas[j] * kka_row + vcols[j] * k_row
            s_ref[b, j, kq] = s_new * q_row if s == SUBLANES - 1 else s_new
            ty = s_new * r_row
            ts = s_new * qkn_row
            acc_y[j] = ty if kq == 0 else acc_y[j] + ty
            acc_s[j] = ts if kq == 0 else acc_s[j] + ts
    for j in js:
        sa_ref[b, j] = _group_sum(acc_s[j])
        yv = jnp.where(diag, _group_sum(acc_y[j]), 0.0)
        yv = yv + pltpu.roll(yv, 1, axis=0)
        yv = yv + pltpu.roll(yv, 2, axis=0)
        y_v[pl.ds(s, 1), pl.ds(2 * j * LANES, LANES)] = yv[K_PER_SLAB - 1:K_PER_SLAB, :]
        y_v[pl.ds(s, 1), pl.ds((2 * j + 1) * LANES, LANES)] = yv[SUBLANES - 1:SUBLANES, :]


def _scan_steps8(views, s_ref, sa_ref, masks):
    for s in range(SUBLANES):
        for b in range(PAIR):
            _scan_step(b, views[b], s, tuple(range(N_VT)), s_ref, sa_ref, masks)


def _scan_body(r_ref, w_ref, k_ref, v_ref, kk_ref, kka_ref, kkh_ref, s0_ref, y_ref, s_ref, sa_ref, *, tt):
    masks = _scan_masks()

    def tile8(ref, b, t8):
        start = t8 * SUBLANES
        if not isinstance(t8, int):
            start = pl.multiple_of(start, SUBLANES)
        return ref.at[b, pl.ds(start, SUBLANES), :]

    @pl.when(pl.program_id(1) == 0)
    def _():
        _scan_init(s0_ref, s_ref, sa_ref, [tile8(kk_ref, b, 0) for b in range(PAIR)])

    n8 = tt // SUBLANES

    def steps8(t8, last):
        views = []
        for b in range(PAIR):
            vb = tuple(tile8(ref, b, t8) for ref in (r_ref, w_ref, k_ref, v_ref, kk_ref, kka_ref))
            nxt = kkh_ref.at[b] if last else tile8(kk_ref, b, t8 + 1)
            views.append(vb + (nxt, tile8(y_ref, b, t8)))
        _scan_steps8(views, s_ref, sa_ref, masks)

    def loop_body(t8, carry):
        steps8(t8, False)
        return carry

    lax.fori_loop(0, n8 - 1, loop_body, 0)
    steps8(n8 - 1, True)


_STATE_BLOCK = (PAIR, N_VT, N_KQ, SUBLANES, LANES)


def _scan(r, w, k, v, kk, kka, s0, *, tt):
    bsz, t_len, d = r.shape
    seq = pl.BlockSpec((PAIR, tt, d), lambda i, j: (i, j, 0))
    st = pl.BlockSpec(_STATE_BLOCK, lambda i, j: (i, 0, 0, 0, 0))
    n8 = tt // SUBLANES
    last8 = t_len // SUBLANES - 1
    head = pl.BlockSpec((PAIR, SUBLANES, d), lambda i, j: (i, jnp.minimum((j + 1) * n8, last8), 0))
    return pl.pallas_call(
        functools.partial(_scan_body, tt=tt),
        grid=(bsz // PAIR, t_len // tt),
        in_specs=[seq] * 6 + [head, st],
        out_specs=[seq, st],
        out_shape=[jax.ShapeDtypeStruct((bsz, t_len, d), F32),
                   jax.ShapeDtypeStruct(s0.shape, F32)],
        scratch_shapes=[pltpu.VMEM((PAIR, N_VT, SUBLANES, LANES), F32)],
        compiler_params=_cparams(("parallel", "arbitrary")),
        name="wkv_scan",
    )(r, w, k, v, kk, kka, kk, s0)


def _state_to_tiles(s):
    b = s.shape[0]
    s = s.reshape(b, N_HEADS, N_VT, SUBLANES, N_KQ, K_PER_SLAB)
    return s.transpose(0, 2, 4, 3, 5, 1).reshape(b, N_VT, N_KQ, SUBLANES, LANES)


def _tiles_to_state(s):
    b = s.shape[0]
    s = s.reshape(b, N_VT, N_KQ, SUBLANES, K_PER_SLAB, N_HEADS)
    return s.transpose(0, 5, 1, 3, 2, 4).reshape(b, N_HEADS, HEAD_DIM, HEAD_DIM)


def _post_body(y_ref, v_ref, bc_ref, gl_ref, gtb_ref, gtc_ref, xc_ref, ga_ref, gb_ref, x1_ref,
               bgc_ref, bxc_ref, lng_ref, lnb_ref, cw_ref, gup_ref, wout_ref, x2_o, ut_o, *, seg):
    tm = y_ref.shape[0]
    nseg = tm // seg
    row = lax.broadcasted_iota(jnp.int32, (tm, 1), 0)
    inv_n = 1.0 / HEAD_DIM

    o = y_ref[...]
    mu = _head_sum(o) * inv_n
    c = o - _tile_slabs(mu)
    var = _head_sum(c * c) * inv_n
    o = c * _tile_slabs(lax.rsqrt(var + GN_EPS)) * lng_ref[...] + lnb_ref[...]
    bonus = _tile_slabs(bc_ref[...]) * v_ref[...]
    g = jnp.dot(gl_ref[...], gup_ref[...], preferred_element_type=F32)
    y_a = (o + bonus) * g

    u = gtc_ref[...] * xc_ref[...]
    ub = bgc_ref[...] * bxc_ref[...]
    u1 = pltpu.roll(u, 1, axis=0)
    u2 = pltpu.roll(u, 2, axis=0)
    for s in range(nseg):
        u1 = jnp.where(row == s * seg, ub[2 * s + 1:2 * s + 2, :], u1)
        u2 = jnp.where(row == s * seg, ub[2 * s:2 * s + 1, :], u2)
        u2 = jnp.where(row == s * seg + 1, ub[2 * s + 1:2 * s + 2, :], u2)
    conv = cw_ref[0:1, :] * u2 + cw_ref[1:2, :] * u1 + cw_ref[2:3, :] * u
    y_b = gtb_ref[...] * conv

    merged = _sigmoid(ga_ref[...]) * y_a + _sigmoid(gb_ref[...]) * y_b
    x2_o[...] = x1_ref[...] + jnp.dot(merged.astype(BF16), wout_ref[...], preferred_element_type=F32)

    ut_o[...] = jnp.zeros_like(ut_o)
    for s in range(nseg):
        ut_o[2 * s:2 * s + 2, :] = u[(s + 1) * seg - 2:(s + 1) * seg, :]


def _post(y, v, bc, gl, p, x1, bgc, bxc, lng, lnb, cw, gup, wout, *, tm, seg):
    m, d = y.shape
    rowb = pl.BlockSpec((tm, d), lambda i: (i, 0))
    grp = lambda gi: pl.BlockSpec((tm, d), lambda i, gi=gi: (i, gi))
    vec = pl.BlockSpec((1, d), lambda i: (0, 0))
    bnd = pl.BlockSpec((SUBLANES, d), lambda i: (i, 0))
    return pl.pallas_call(
        functools.partial(_post_body, seg=seg),
        grid=(m // tm,),
        in_specs=[rowb, rowb, pl.BlockSpec((tm, LANES), lambda i: (i, 0)),
                  pl.BlockSpec((tm, LORA_G), lambda i: (i, 0)),
                  grp(3), grp(4), grp(5), grp(6), grp(7), rowb,
                  bnd, bnd, vec, vec,
                  pl.BlockSpec((CONV_W, d), lambda i: (0, 0)),
                  pl.BlockSpec((LORA_G, d), lambda i: (0, 0)),
                  pl.BlockSpec((d, d), lambda i: (0, 0), pipeline_mode=pl.Buffered(1))],
        out_specs=[rowb, bnd],
        out_shape=[jax.ShapeDtypeStruct((m, d), F32),
                   jax.ShapeDtypeStruct((m // tm * SUBLANES, d), F32)],
        compiler_params=_cparams(("parallel",)),
        name="post",
    )(y, v, bc, gl, p, p, p, p, p, x1, bgc, bxc, lng, lnb, cw, gup, wout)


def _hm(x):
    lead = x.shape[:-1]
    return x.reshape(*lead, N_HEADS, HEAD_DIM).swapaxes(-1, -2).reshape(*lead, D_MODEL)


def _hm_groups(x):
    lead = x.shape[:-1]
    n = x.shape[-1] // D_MODEL
    return x.reshape(*lead, n, N_HEADS, HEAD_DIM).swapaxes(-1, -2).reshape(*lead, n * D_MODEL)


def _hm_inv(x):
    lead = x.shape[:-1]
    return x.reshape(*lead, HEAD_DIM, N_HEADS).swapaxes(-1, -2).reshape(*lead, D_MODEL)


def _pad_lora(x):
    z = jnp.zeros(x.shape[:-1] + (LANES - LORA_W,), x.dtype)
    return jnp.concatenate([x[..., :LORA_W], z, x[..., LORA_W:LORA_W + LORA_A], z,
                            x[..., LORA_W + LORA_A:]], axis=-1)


def _pad_rows(x, rows):
    return jnp.concatenate([x, jnp.zeros((rows - x.shape[0],) + x.shape[1:], x.dtype)], axis=0)


def _tile_bounds(rows, n_tiles, n_used):
    w = rows.shape[-1]
    rows = rows.reshape(n_tiles, n_used, w)
    pad = jnp.zeros((n_tiles, SUBLANES - n_used, w), rows.dtype)
    return jnp.concatenate([rows, pad], axis=1).reshape(n_tiles * SUBLANES, w)


def _front(x, shift_prev, wkv_prev, wts, *, tm_prep, seg_prep):
    bsz, t_len, d = x.shape
    m = bsz * t_len
    x1, h, h_last = _ffn(x.reshape(m, d), wts["ffn1_norm"], wts["ffn1_gate"], wts["ffn1_up"], wts["ffn1_down"],
                         wts["mix_norm"], seg=min(t_len, FFN_TM))
    p = _proj(h, wts["w_groups"], tm=min(m, PROJ_TM))

    n_tiles = m // tm_prep
    if shift_prev is None:
        starts = jnp.arange(n_tiles) * tm_prep
        prev = jnp.take(p, jnp.maximum(starts - 1, 0), axis=0)
        prev = jnp.where((starts % t_len == 0)[:, None], 0.0, prev)
        bnd = jnp.concatenate([prev[:, :3 * d], prev[:, N_GROUPS * d:]], axis=1)
        n_used = 1
    else:
        rows = shift_prev.shape[0]
        ws = wts["w_groups"]
        bnd = _proj(_pad_rows(shift_prev, SHIFT_ROWS).astype(BF16), [ws[0], ws[2]], tm=SHIFT_ROWS)[:rows]
        n_used = tm_prep // seg_prep
    bnd = _tile_bounds(bnd, n_tiles, n_used)

    r, w, k, v, kk, kka, gl, bc = _prep(p, bnd, wts["prep_vecs"], wts["prep_ups"], tm=tm_prep, seg=seg_prep)
    sh = (bsz, t_len, d)
    scan_in = tuple(a.reshape(sh) for a in (r, w, k, v, kk, kka)) + (_state_to_tiles(wkv_prev),)

    segs = m // min(t_len, FFN_TM)
    h_last = h_last.reshape(m // FFN_TM, SUBLANES, d)[:, :segs // (m // FFN_TM)].reshape(segs, d)
    shift_new = h_last.reshape(bsz, segs // bsz, d)[:, -1]
    return dict(x1=x1, p=p, v=v, gl=gl, bc=bc, scan_in=scan_in, shift_new=shift_new, shape=sh)


def _merge(fr, y, conv_prev, wts, *, tm_post, seg_post):
    bsz, t_len, d = fr["shape"]
    m = bsz * t_len
    p = fr["p"]
    n_tiles = m // tm_post
    if conv_prev is None:
        starts = jnp.arange(n_tiles) * tm_post
        idx = jnp.maximum(starts[:, None] + jnp.array([-2, -1])[None, :], 0).reshape(-1)
        rows = jnp.take(p, idx, axis=0)
        live = jnp.repeat(starts % t_len != 0, 2)[:, None]
        bgc = jnp.where(live, rows[:, 4 * d:5 * d], 0.0)
        bxc = rows[:, 5 * d:6 * d]
        n_used = 2
    else:
        bgc = _hm(conv_prev).reshape(bsz * (CONV_W - 1), d)
        bxc = jnp.ones_like(bgc)
        n_used = 2 * (tm_post // seg_post)
    bgc = _tile_bounds(bgc, n_tiles, n_used)
    bxc = _tile_bounds(bxc, n_tiles, n_used)

    x2, utail = _post(y.reshape(m, d), fr["v"], fr["bc"], fr["gl"], p, fr["x1"], bgc, bxc,
                      wts["ln_g"], wts["ln_b"], wts["conv_w"], wts["g_up"], wts["w_out"],
                      tm=tm_post, seg=seg_post)
    nseg_post = tm_post // seg_post
    ut = utail.reshape(n_tiles, SUBLANES, d)[:, :2 * nseg_post]
    ut = ut.reshape(n_tiles * nseg_post, 2, d)
    segs_per_seq = t_len // seg_post
    return x2, _hm_inv(ut[segs_per_seq - 1::segs_per_seq])


def kernel(x_prompt, x_sample, state_shift, state_conv, state_wkv, ffn1_norm, ffn1_gate, ffn1_up, ffn1_down, mix_norm, w_in, mu_shift, w0, w_lora_up, a0, a_lora_up, g_lora_up, k_k, k_a, r_k, ln_x_g, ln_x_b, conv_w, w_out, ffn2_norm, ffn2_gate, ffn2_up, ffn2_down, final_norm):
    depth = w_in.shape[0]
    assert depth == 1, "single-layer kernel"
    d = D_MODEL
    n_a = 3 * d + LORA_W + LORA_A + LORA_G
    l = 0

    wi = w_in[l]
    w_groups = [_hm_groups(wi[:, :3 * d].astype(BF16)), _hm_groups(wi[:, n_a:].astype(BF16)),
                _pad_lora(wi[:, 3 * d:n_a].astype(BF16))]

    mu = mu_shift[l]
    row = lambda x: x.reshape(1, -1)
    prep_vecs = (row(_hm(mu[0:d])), row(_hm(mu[d:2 * d])), row(_hm(mu[2 * d:3 * d])),
                 row(_pad_lora(mu[3 * d:])),
                 row(_hm(w0[l])), row(_hm(a0[l])), row(_hm(k_k[l])), row(_hm(k_a[l])),
                 row(_hm(r_k[l].reshape(d))))
    prep_ups = (_pad_rows(_hm(w_lora_up[l]), LANES).astype(BF16),
                _pad_rows(_hm(a_lora_up[l]), LANES).astype(BF16))

    wts = dict(
        ffn1_norm=row(ffn1_norm[l]), ffn1_gate=ffn1_gate[l].astype(BF16), ffn1_up=ffn1_up[l].astype(BF16),
        ffn1_down=ffn1_down[l].astype(BF16), mix_norm=row(mix_norm[l]), w_groups=w_groups,
        prep_vecs=prep_vecs, prep_ups=prep_ups, g_up=_hm(g_lora_up[l]).astype(BF16),
        ln_g=row(_hm(ln_x_g[l])), ln_b=row(_hm(ln_x_b[l])), conv_w=_hm(conv_w[l]),
        w_out=_hm(w_out[l].T).T.astype(BF16),
        ffn2_norm=row(ffn2_norm[l]), ffn2_gate=ffn2_gate[l].astype(BF16), ffn2_up=ffn2_up[l].astype(BF16),
        ffn2_down=ffn2_down[l].astype(BF16), final_norm=row(final_norm),
    )

    b_p, t_p, _ = x_prompt.shape
    b_s, t_s, _ = x_sample.shape
    ffn2 = (wts["ffn2_norm"], wts["ffn2_gate"], wts["ffn2_up"], wts["ffn2_down"], wts["final_norm"])
    zero_state = jnp.zeros((b_p, N_HEADS, HEAD_DIM, HEAD_DIM), state_wkv.dtype)

    fp = _front(x_prompt, None, zero_state, wts, tm_prep=PREP_TM, seg_prep=PREP_TM)
    y_p, s_p = _scan(*fp["scan_in"], tt=min(t_p, SCAN_TT))
    x2_p, cv_p = _merge(fp, y_p, None, wts, tm_post=POST_TM, seg_post=POST_TM)
    (yo_p,) = _ffn(x2_p, *ffn2, seg=None)

    fs = _front(x_sample, state_shift[l], state_wkv[l], wts, tm_prep=PREP_TM, seg_prep=t_s)
    y_s, s_s = _scan(*fs["scan_in"], tt=min(t_s, SCAN_TT))
    x2_s, cv_s = _merge(fs, y_s, state_conv[l], wts, tm_post=POST_TM, seg_post=t_s)
    (yo_s,) = _ffn(x2_s, *ffn2, seg=None)

    return (yo_p.reshape(x_prompt.shape), yo_s.reshape(x_sample.shape),
            fp["shift_new"][None], cv_p[None], _tiles_to_state(s_p)[None],
            fs["shift_new"][None], cv_s[None], _tiles_to_state(s_s)[None])
```
